```python
import math
import jax, jax.numpy as jnp
from jax import lax
import numpy as np

D_MODEL = 1024
BATCH = 8
SEQ = 2048
DEPTH = 1

D_CONV = 1024
CONV_K = 31
D_RNN = 1280
LRU_HEADS = 16
LRU_BLOCK = D_RNN // LRU_HEADS
LRU_CONV_K = 4
LRU_C = 8.0
D_FF = ((int(math.ceil(8 * D_MODEL / 3)) + 255) // 256) * 256
IN_SPLITS = (2 * D_CONV, D_RNN, D_RNN, D_MODEL, D_MODEL)
D_IN = sum(IN_SPLITS)
EPS = 1e-6

kernel_name = "hybrid_conformer_conv_rglru_gated_block"


def rmsnorm(x, g):
    xf = x.astype(jnp.float32)
    y = xf * lax.rsqrt(jnp.mean(xf * xf, axis=-1, keepdims=True) + EPS)
    return (y * g.astype(jnp.float32)).astype(x.dtype)


def layernorm(x, g, b):
    xf = x.astype(jnp.float32)
    mu = jnp.mean(xf, axis=-1, keepdims=True)
    xc = xf - mu
    var = jnp.mean(xc * xc, axis=-1, keepdims=True)
    y = xc * lax.rsqrt(var + EPS) * g.astype(jnp.float32) + b.astype(jnp.float32)
    return y.astype(x.dtype)


def causal_depthwise_conv(u, w, b):
    k, c = w.shape
    u_pad = jnp.pad(u, ((0, 0), (k - 1, 0), (0, 0)))
    y = lax.conv_general_dilated(
        u_pad, w[:, None, :].astype(u.dtype), window_strides=(1,), padding='VALID',
        dimension_numbers=('NWC', 'WIO', 'NWC'), feature_group_count=c)
    return y + b.astype(u.dtype)


def _linear_recurrence_combine(left, right):
    a1, b1 = left
    a2, b2 = right
    return a1 * a2, a2 * b1 + b2


def rg_lru(xb, wa, ba, wx, bx, lam):
    bsz, s, d = xb.shape
    xf = xb.astype(jnp.float32)
    xh = xf.reshape(bsz, s, LRU_HEADS, LRU_BLOCK)
    r = jax.nn.sigmoid(jnp.einsum('bshi,hij->bshj', xh, wa.astype(jnp.float32)).reshape(bsz, s, d)
                       + ba.astype(jnp.float32))
    i = jax.nn.sigmoid(jnp.einsum('bshi,hij->bshj', xh, wx.astype(jnp.float32)).reshape(bsz, s, d)
                       + bx.astype(jnp.float32))
    log_a = -LRU_C * r * jax.nn.softplus(-lam.astype(jnp.float32))
    a = jnp.exp(log_a)
    mult = jnp.sqrt(-jnp.expm1(2.0 * log_a))
    bterm = mult * (i * xf)
    _, h = lax.associative_scan(_linear_recurrence_combine, (a, bterm), axis=1)
    return h.astype(xb.dtype)


def setup_inputs(seed: int = 0) -> dict:
    key = jax.random.key(seed)
    ks = jax.random.split(key, 32)
    f32 = jnp.float32
    L = DEPTH

    def nrm(k, shape, fan_in, scale=1.0):
        return jax.random.normal(k, shape, f32) * (scale * fan_in ** -0.5)

    def small(k, shape, s=0.02):
        return jax.random.normal(k, shape, f32) * s

    x = jax.random.normal(ks[0], (BATCH, SEQ, D_MODEL), f32)
    a0 = jax.random.uniform(ks[16], (L, D_RNN), f32, 0.9, 0.999)
    lru_lambda = jnp.log(a0) - jnp.log1p(-a0)
    return {
        "x": x,
        "norm1_g": 1.0 + small(ks[1], (L, D_MODEL)),
        "w_in": nrm(ks[2], (L, D_MODEL, D_IN), D_MODEL),
        "b_in": small(ks[3], (L, D_IN)),
        "conv_dw_w": nrm(ks[4], (L, CONV_K, D_CONV), CONV_K),
        "conv_dw_b": small(ks[5], (L, D_CONV)),
        "conv_ln_g": 1.0 + small(ks[6], (L, D_CONV)),
        "conv_ln_b": small(ks[7], (L, D_CONV)),
        "conv_w_out": nrm(ks[8], (L, D_CONV, D_MODEL), D_CONV),
        "conv_b_out": small(ks[9], (L, D_MODEL)),
        "lru_conv_w": nrm(ks[10], (L, LRU_CONV_K, D_RNN), LRU_CONV_K),
        "lru_conv_b": small(ks[11], (L, D_RNN)),
        "lru_wa": nrm(ks[12], (L, LRU_HEADS, LRU_BLOCK, LRU_BLOCK), LRU_BLOCK),
        "lru_ba": small(ks[13], (L, D_RNN)),
        "lru_wx": nrm(ks[14], (L, LRU_HEADS, LRU_BLOCK, LRU_BLOCK), LRU_BLOCK),
        "lru_bx": small(ks[15], (L, D_RNN)),
        "lru_lambda": lru_lambda,
        "lru_w_out": nrm(ks[17], (L, D_RNN, D_MODEL), D_RNN),
        "w_mix_out": nrm(ks[18], (L, D_MODEL, D_MODEL), D_MODEL),
        "norm2_g": 1.0 + small(ks[19], (L, D_MODEL)),
        "ffn_w1": nrm(ks[20], (L, D_MODEL, D_FF), D_MODEL),
        "ffn_w3": nrm(ks[21], (L, D_MODEL, D_FF), D_MODEL),
        "ffn_w2": nrm(ks[22], (L, D_FF, D_MODEL), D_FF),
        "norm_f_g": 1.0 + small(ks[23], (D_MODEL,)),
    }


def reference(x, norm1_g, w_in, b_in, conv_dw_w, conv_dw_b, conv_ln_g, conv_ln_b,
              conv_w_out, conv_b_out, lru_conv_w, lru_conv_b, lru_wa, lru_ba, lru_wx,
              lru_bx, lru_lambda, lru_w_out, w_mix_out, norm2_g, ffn_w1, ffn_w3, ffn_w2,
              norm_f_g):
    split_idx = list(np.cumsum(IN_SPLITS)[:-1])
    for l in range(DEPTH):
        h = rmsnorm(x, norm1_g[l])
        z = h @ w_in[l] + b_in[l]
        z_glu, z_rx, z_rg, z_ga, z_gb = jnp.split(z, split_idx, axis=-1)

        u = z_glu[..., :D_CONV] * jax.nn.sigmoid(z_glu[..., D_CONV:])
        u = causal_depthwise_conv(u, conv_dw_w[l], conv_dw_b[l])
        u = jax.nn.silu(layernorm(u, conv_ln_g[l], conv_ln_b[l]))
        y_a = u @ conv_w_out[l] + conv_b_out[l]

        xb = causal_depthwise_conv(z_rx, lru_conv_w[l], lru_conv_b[l])
        hb = rg_lru(xb, lru_wa[l], lru_ba[l], lru_wx[l], lru_bx[l], lru_lambda[l])
        y_b = (hb * jax.nn.gelu(z_rg, approximate=True)) @ lru_w_out[l]

        m = jax.nn.sigmoid(z_ga) * y_a + jax.nn.sigmoid(z_gb) * y_b
        x = x + m @ w_mix_out[l]

        h = rmsnorm(x, norm2_g[l])
        x = x + (jax.nn.silu(h @ ffn_w1[l]) * (h @ ffn_w3[l])) @ ffn_w2[l]
    return rmsnorm(x, norm_f_g)
```

```python
import functools
import math

import jax
import jax.numpy as jnp
from jax import lax
from jax.experimental import pallas as pl
from jax.experimental.pallas import tpu as pltpu

EPS = 1e-6
LRU_C = 8.0

SUBLANES = 8
LANES = 128
MXU_COLS = 256

MIX_TILE = 256
FFN_TILE = 512
CONV_ROWS = 64
NORM_ROWS = 32
GATE_K = 512

MIX_VMEM_BYTES = 48 * 1024 * 1024
FFN_VMEM_BYTES = 44 * 1024 * 1024


def _sigmoid(x):
    return 0.5 * jnp.tanh(0.5 * x) + 0.5


def _silu(x):
    return x * _sigmoid(x)


def _gelu_tanh(x):
    c = math.sqrt(2.0 / math.pi)
    return 0.5 * x * (1.0 + jnp.tanh(c * (x + 0.044715 * (x * x * x))))


def _dot(a, b):
    return jnp.dot(a, b, preferred_element_type=jnp.float32)


def _gate_windows(d_rnn, block):
    starts = []
    for j in range(d_rnn // MXU_COLS):
        lo = (j * MXU_COLS // block) * block
        hi = -(-((j + 1) * MXU_COLS) // block) * block
        start = min(lo // LANES * LANES, d_rnn - GATE_K)
        assert start <= lo and hi <= start + GATE_K
        starts.append(start)
    return tuple(starts)


def _mixer_kernel(x_ref, n1g_ref, win_ref, bin_ref, cw_ref, cb_ref, lng_ref, lnb_ref, cwo_ref, cbo_ref,
                  lcw_ref, lcb_ref, wg_ref, ba_ref, bx_ref, lam_ref, lwo_ref, wmix_ref,
                  o_ref,
                  h_scr, u_scr, c_scr, v_scr, rx_scr, xb_scr, xbb_scr, y_scr, m_scr, mb_scr, hcar_scr,
                  *, gate_starts):
    T, D = x_ref.shape
    d_conv = cwo_ref.shape[0]
    d_rnn = lwo_ref.shape[0]
    conv_k = cw_ref.shape[0]
    lru_k = lcw_ref.shape[0]
    u_hist = u_scr.shape[0] - T
    rx_hist = rx_scr.shape[0] - T
    off_glu_b = d_conv
    off_rx = 2 * d_conv
    off_rg = off_rx + d_rnn
    off_ga = off_rg + d_rnn
    off_gb = off_ga + D

    @pl.when(pl.program_id(1) == 0)
    def _():
        u_scr[0:u_hist, :] = jnp.zeros((u_hist, d_conv), jnp.float32)
        rx_scr[0:rx_hist, :] = jnp.zeros((rx_hist, d_rnn), jnp.float32)
        hcar_scr[...] = jnp.zeros(hcar_scr.shape, jnp.float32)

    for r in range(0, T, NORM_ROWS):
        x = x_ref[r:r + NORM_ROWS, :]
        ms = jnp.mean(x * x, axis=-1, keepdims=True)
        h_scr[r:r + NORM_ROWS, :] = (x * lax.rsqrt(ms + EPS) * n1g_ref[...]).astype(jnp.bfloat16)

    def in_proj(off, c0, width=MXU_COLS):
        cols = slice(off + c0, off + c0 + width)
        return _dot(h_scr[...], win_ref[:, cols]) + bin_ref[:, cols]

    for c0 in range(0, d_conv, MXU_COLS):
        za = in_proj(0, c0)
        zb = in_proj(off_glu_b, c0)
        u_scr[u_hist:u_hist + T, c0:c0 + MXU_COLS] = za * _sigmoid(zb)

    for l0 in range(0, d_conv, LANES):
        lanes = slice(l0, l0 + LANES)
        for r in range(0, T, CONV_ROWS):
            acc = jnp.broadcast_to(cb_ref[:, lanes], (CONV_ROWS, LANES))
            for k in range(conv_k):
                row = u_hist + r - (conv_k - 1) + k
                acc = acc + cw_ref[k:k + 1, lanes] * u_scr[row:row + CONV_ROWS, lanes]
            c_scr[r:r + CONV_ROWS, lanes] = acc
    u_scr[0:u_hist, :] = u_scr[T:T + u_hist, :]

    for r in range(0, T, NORM_ROWS):
        cv = c_scr[r:r + NORM_ROWS, :]
        mu = jnp.mean(cv, axis=-1, keepdims=True)
        xc = cv - mu
        var = jnp.mean(xc * xc, axis=-1, keepdims=True)
        y = xc * lax.rsqrt(var + EPS) * lng_ref[...] + lnb_ref[...]
        v_scr[r:r + NORM_ROWS, :] = _silu(y).astype(jnp.bfloat16)

    for c0 in range(0, D, MXU_COLS):
        cols = slice(c0, c0 + MXU_COLS)
        ya = _dot(v_scr[...], cwo_ref[:, cols]) + cbo_ref[:, cols]
        m_scr[:, cols] = _sigmoid(in_proj(off_ga, c0)) * ya

    for c0 in range(0, d_rnn, MXU_COLS):
        rx_scr[rx_hist:rx_hist + T, c0:c0 + MXU_COLS] = in_proj(off_rx, c0)

    for l0 in range(0, d_rnn, LANES):
        lanes = slice(l0, l0 + LANES)
        for r in range(0, T, CONV_ROWS):
            acc = jnp.broadcast_to(lcb_ref[:, lanes], (CONV_ROWS, LANES))
            for k in range(lru_k):
                row = rx_hist + r - (lru_k - 1) + k
                acc = acc + lcw_ref[k:k + 1, lanes] * rx_scr[row:row + CONV_ROWS, lanes]
            xb_scr[r:r + CONV_ROWS, lanes] = acc
            xbb_scr[r:r + CONV_ROWS, lanes] = acc.astype(jnp.bfloat16)
    rx_scr[0:rx_hist, :] = rx_scr[T:T + rx_hist, :]

    lam = lam_ref[...]
    neg_c_sp = -LRU_C * (jnp.maximum(-lam, 0.0) + jnp.log1p(jnp.exp(-jnp.abs(lam))))

    groups = T // SUBLANES
    sub = lax.broadcasted_iota(jnp.int32, (groups, SUBLANES, MXU_COLS), 1)
    for j, ks in enumerate(gate_starts):
        c0 = j * MXU_COLS
        cols = slice(c0, c0 + MXU_COLS)
        g = _dot(xbb_scr[:, ks:ks + GATE_K], wg_ref[j])
        r_gate = _sigmoid(g[:, :MXU_COLS] + ba_ref[:, cols])
        i_gate = _sigmoid(g[:, MXU_COLS:] + bx_ref[:, cols])
        log_a = neg_c_sp[:, cols] * r_gate
        a = jnp.exp(log_a)
        mult = jnp.sqrt(-jnp.tanh(log_a) * (1.0 + a * a))
        b = mult * (i_gate * xb_scr[:, cols])

        a3 = a.reshape(groups, SUBLANES, MXU_COLS)
        b3 = b.reshape(groups, SUBLANES, MXU_COLS)
        d = 1
        while d < SUBLANES:
            keep = sub >= d
            a_sh = jnp.where(keep, pltpu.roll(a3, d, 1), 1.0)
            b_sh = jnp.where(keep, pltpu.roll(b3, d, 1), 0.0)
            b3 = b3 + a3 * b_sh
            a3 = a3 * a_sh
            d *= 2
        carry = hcar_scr[:, cols]
        hs = []
        for gi in range(groups):
            hg = b3[gi] + a3[gi] * carry
            hs.append(hg)
            carry = jnp.broadcast_to(hg[SUBLANES - 1:SUBLANES, :], (SUBLANES, MXU_COLS))
        hcar_scr[:, cols] = carry
        hseq = jnp.concatenate(hs, axis=0)

        y_scr[:, cols] = (hseq * _gelu_tanh(in_proj(off_rg, c0))).astype(jnp.bfloat16)

    for c0 in range(0, D, MXU_COLS):
        cols = slice(c0, c0 + MXU_COLS)
        yb = _dot(y_scr[...], lwo_ref[:, cols])
        mb_scr[:, cols] = (m_scr[:, cols] + _sigmoid(in_proj(off_gb, c0)) * yb).astype(jnp.bfloat16)

    for c0 in range(0, D, MXU_COLS):
        cols = slice(c0, c0 + MXU_COLS)
        o_ref[:, cols] = x_ref[:, cols] + _dot(mb_scr[...], wmix_ref[:, cols])


def _ffn_kernel(x_ref, n2g_ref, w1_ref, w3_ref, w2_ref, nfg_ref, o_ref, h_scr, act_scr, y_scr):
    tm, D = x_ref.shape
    d_ff = w1_ref.shape[1]

    for r in range(0, tm, NORM_ROWS):
        x = x_ref[r:r + NORM_ROWS, :]
        ms = jnp.mean(x * x, axis=-1, keepdims=True)
        h_scr[r:r + NORM_ROWS, :] = (x * lax.rsqrt(ms + EPS) * n2g_ref[...]).astype(jnp.bfloat16)

    for c0 in range(0, d_ff, MXU_COLS):
        cols = slice(c0, c0 + MXU_COLS)
        g = _dot(h_scr[...], w1_ref[:, cols])
        u = _dot(h_scr[...], w3_ref[:, cols])
        act_scr[:, cols] = (_silu(g) * u).astype(jnp.bfloat16)

    for c0 in range(0, D, MXU_COLS):
        cols = slice(c0, c0 + MXU_COLS)
        y_scr[:, cols] = x_ref[:, cols] + _dot(act_scr[...], w2_ref[:, cols])

    for r in range(0, tm, NORM_ROWS):
        y = y_scr[r:r + NORM_ROWS, :]
        ms = jnp.mean(y * y, axis=-1, keepdims=True)
        o_ref[r:r + NORM_ROWS, :] = y * lax.rsqrt(ms + EPS) * nfg_ref[...]


def _resident(shape):
    return pl.BlockSpec(shape, lambda *_: (0,) * len(shape), pipeline_mode=pl.Buffered(1))


def _gate_weights(wa, wx, gate_starts):
    heads, blk, _ = wa.shape
    d_rnn = heads * blk
    eye = jnp.eye(heads, dtype=wa.dtype)

    def dense(w):
        return (eye[:, None, :, None] * w[:, :, None, :]).reshape(d_rnn, d_rnn)

    da, dx = dense(wa), dense(wx)
    tiles = []
    for j, ks in enumerate(gate_starts):
        cols = slice(j * MXU_COLS, (j + 1) * MXU_COLS)
        tiles.append(jnp.concatenate([da[ks:ks + GATE_K, cols], dx[ks:ks + GATE_K, cols]], axis=1))
    return jnp.stack(tiles).astype(jnp.bfloat16)


def kernel(x, norm1_g, w_in, b_in, conv_dw_w, conv_dw_b, conv_ln_g, conv_ln_b, conv_w_out, conv_b_out,
           lru_conv_w, lru_conv_b, lru_wa, lru_ba, lru_wx, lru_bx, lru_lambda, lru_w_out, w_mix_out,
           norm2_g, ffn_w1, ffn_w3, ffn_w2, norm_f_g):
    B, S, D = x.shape
    depth = w_in.shape[0]
    d_in = w_in.shape[2]
    conv_k, d_conv = conv_dw_w.shape[1:]
    lru_k, d_rnn = lru_conv_w.shape[1:]
    heads, blk = lru_wa.shape[1:3]
    d_ff = ffn_w1.shape[2]
    T = MIX_TILE
    assert S % T == 0 and (B * S) % FFN_TILE == 0
    assert d_in == 2 * d_conv + 2 * d_rnn + 2 * D and heads * blk == d_rnn
    assert d_conv % MXU_COLS == 0 and d_rnn % MXU_COLS == 0 and D % MXU_COLS == 0 and d_ff % MXU_COLS == 0
    u_hist = -(-(conv_k - 1) // SUBLANES) * SUBLANES
    rx_hist = -(-(lru_k - 1) // SUBLANES) * SUBLANES
    gate_starts = _gate_windows(d_rnn, blk)
    n_gate = len(gate_starts)
    bf16 = jnp.bfloat16
    f32 = jnp.float32

    def row(v):
        return v.reshape(1, -1)

    for l in range(depth):
        wg = _gate_weights(lru_wa[l], lru_wx[l], gate_starts)
        mixer = pl.pallas_call(
            functools.partial(_mixer_kernel, gate_starts=gate_starts),
            grid=(B, S // T),
            in_specs=[
                pl.BlockSpec((None, T, D), lambda b, s: (b, s, 0)),
                _resident((1, D)),
                _resident((D, d_in)), _resident((1, d_in)),
                _resident((conv_k, d_conv)), _resident((1, d_conv)),
                _resident((1, d_conv)), _resident((1, d_conv)),
                _resident((d_conv, D)), _resident((1, D)),
                _resident((lru_k, d_rnn)), _resident((1, d_rnn)),
                _resident((n_gate, GATE_K, 2 * MXU_COLS)),
                _resident((1, d_rnn)), _resident((1, d_rnn)), _resident((1, d_rnn)),
                _resident((d_rnn, D)), _resident((D, D)),
            ],
            out_specs=pl.BlockSpec((None, T, D), lambda b, s: (b, s, 0)),
            out_shape=jax.ShapeDtypeStruct((B, S, D), f32),
            scratch_shapes=[
                pltpu.VMEM((T, D), bf16),
                pltpu.VMEM((u_hist + T, d_conv), f32),
                pltpu.VMEM((T, d_conv), f32),
                pltpu.VMEM((T, d_conv), bf16),
                pltpu.VMEM((rx_hist + T, d_rnn), f32),
                pltpu.VMEM((T, d_rnn), f32),
                pltpu.VMEM((T, d_rnn), bf16),
                pltpu.VMEM((T, d_rnn), bf16),
                pltpu.VMEM((T, D), f32),
                pltpu.VMEM((T, D), bf16),
                pltpu.VMEM((SUBLANES, d_rnn), f32),
            ],
            compiler_params=pltpu.CompilerParams(
                dimension_semantics=("arbitrary", "arbitrary"),
                vmem_limit_bytes=MIX_VMEM_BYTES),
            name="token_mixer",
        )
        x = mixer(
            x, row(norm1_g[l]), w_in[l].astype(bf16), row(b_in[l]),
            conv_dw_w[l], row(conv_dw_b[l]), row(conv_ln_g[l]), row(conv_ln_b[l]),
            conv_w_out[l].astype(bf16), row(conv_b_out[l]),
            lru_conv_w[l], row(lru_conv_b[l]), wg, row(lru_ba[l]), row(lru_bx[l]), row(lru_lambda[l]),
            lru_w_out[l].astype(bf16), w_mix_out[l].astype(bf16))

        last = l == depth - 1
        ffn = pl.pallas_call(
            _ffn_kernel,
            grid=(B * S // FFN_TILE,),
            in_specs=[
                pl.BlockSpec((FFN_TILE, D), lambda i: (i, 0)),
                _resident((1, D)),
                _resident((D, d_ff)), _resident((D, d_ff)), _resident((d_ff, D)),
                _resident((1, D)),
            ],
            out_specs=pl.BlockSpec((FFN_TILE, D), lambda i: (i, 0)),
            out_shape=jax.ShapeDtypeStruct((B * S, D), f32),
            scratch_shapes=[
                pltpu.VMEM((FFN_TILE, D), bf16),
                pltpu.VMEM((FFN_TILE, d_ff), bf16),
                pltpu.VMEM((FFN_TILE, D), f32),
            ],
            compiler_params=pltpu.CompilerParams(
                dimension_semantics=("arbitrary",),
                vmem_limit_bytes=FFN_VMEM_BYTES),
            name="channel_mixer",
        )
        assert last, "the final rmsnorm is fused into the last layer's channel mixer"
        x = ffn(x.reshape(B * S, D), row(norm2_g[l]), ffn_w1[l].astype(bf16), ffn_w3[l].astype(bf16),
                ffn_w2[l].astype(bf16), row(norm_f_g)).reshape(B, S, D)
    return x
```

```python
import functools
import math

import jax
import jax.numpy as jnp
from jax import lax
from jax.experimental import pallas as pl
from jax.experimental.pallas import tpu as pltpu

EPS = 1e-6
LRU_C = 8.0

SUBLANES = 8
LANES = 128
MXU_COLS = 256

MIX_TILE = 256
FFN_TILE = 512
CONV_ROWS = 64
NORM_ROWS = 32
GATE_K = 512

MIX_VMEM_BYTES = 48 * 1024 * 1024
FFN_VMEM_BYTES = 44 * 1024 * 1024


def _sigmoid(x):
    return 0.5 * jnp.tanh(0.5 * x) + 0.5


def _silu(x):
    return x * _sigmoid(x)


def _gelu_tanh(x):
    c = math.sqrt(2.0 / math.pi)
    return 0.5 * x * (1.0 + jnp.tanh(c * (x + 0.044715 * (x * x * x))))


def _dot(a, b):
    return jnp.dot(a, b, preferred_element_type=jnp.float32)


def _gate_windows(d_rnn, block):
    starts = []
    for j in range(d_rnn // MXU_COLS):
        lo = (j * MXU_COLS // block) * block
        hi = -(-((j + 1) * MXU_COLS) // block) * block
        start = min(lo // LANES * LANES, d_rnn - GATE_K)
        assert start <= lo and hi <= start + GATE_K
        starts.append(start)
    return tuple(starts)


def _mixer_kernel(x_ref, n1g_ref, win_ref, bin_ref, cw_ref, cb_ref, lng_ref, lnb_ref, cwo_ref, cbo_ref,
                  lcw_ref, lcb_ref, wg_ref, ba_ref, bx_ref, lam_ref, lwo_ref, wmix_ref,
                  o_ref,
                  hp_scr, h_scr, u_scr, uprev_scr, c_scr, v_scr, rx_scr, rxprev_scr, xb_scr, xbb_scr,
                  y_scr, m_scr, mb_scr, op_scr, hcar_scr,
                  *, gate_starts):
    T, D = x_ref.shape
    L = T // SUBLANES
    d_conv = cwo_ref.shape[0]
    d_rnn = lwo_ref.shape[0]
    u_halo = cw_ref.shape[0] - 1
    rx_halo = lcw_ref.shape[0] - 1
    assert u_halo <= L and rx_halo <= L and L % NORM_ROWS == 0 and T % CONV_ROWS == 0
    off_glu_b = d_conv
    off_rx = 2 * d_conv
    off_rg = off_rx + d_rnn
    off_ga = off_rg + d_rnn
    off_gb = off_ga + D

    @pl.when(pl.program_id(1) == 0)
    def _():
        uprev_scr[...] = jnp.zeros(uprev_scr.shape, jnp.float32)
        rxprev_scr[...] = jnp.zeros(rxprev_scr.shape, jnp.float32)
        hcar_scr[...] = jnp.zeros(hcar_scr.shape, jnp.float32)

    def interleaved_rows(r, n):
        s, i0 = divmod(r, L)
        return pl.ds(SUBLANES * i0 + s, n, stride=SUBLANES)

    for r in range(0, T, NORM_ROWS):
        x = x_ref[r:r + NORM_ROWS, :]
        ms = jnp.mean(x * x, axis=-1, keepdims=True)
        h = x * lax.rsqrt(ms + EPS) * n1g_ref[...]
        for c in range(D // LANES):
            hp_scr[c, interleaved_rows(r, NORM_ROWS), :] = h[:, c * LANES:(c + 1) * LANES]
    for c in range(D // LANES):
        h_scr[:, c * LANES:(c + 1) * LANES] = hp_scr[c].astype(jnp.bfloat16)

    def in_proj(off, c0, width=MXU_COLS):
        cols = slice(off + c0, off + c0 + width)
        return _dot(h_scr[...], win_ref[:, cols]) + bin_ref[:, cols]

    def fill_halo(buf, prev, halo):
        n, width = SUBLANES * halo, buf.shape[1]
        tail = buf[T:T + n, :].reshape(halo, SUBLANES, width)
        sub = lax.broadcasted_iota(jnp.int32, tail.shape, 1)
        mixed = jnp.where(sub == SUBLANES - 1, prev[...].reshape(tail.shape), tail)
        buf[0:n, :] = pltpu.roll(mixed, 1, 1).reshape(n, width)
        prev[...] = tail.reshape(n, width)

    def causal_conv(buf, w_ref, b_ref, emit):
        for l0 in range(0, buf.shape[1], LANES):
            lanes = slice(l0, l0 + LANES)
            for r in range(0, T, CONV_ROWS):
                acc = jnp.broadcast_to(b_ref[:, lanes], (CONV_ROWS, LANES))
                for k in range(w_ref.shape[0]):
                    row = r + SUBLANES * k
                    acc = acc + w_ref[k:k + 1, lanes] * buf[row:row + CONV_ROWS, lanes]
                emit(r, lanes, acc)

    u0 = SUBLANES * u_halo
    for c0 in range(0, d_conv, MXU_COLS):
        za = in_proj(0, c0)
        zb = in_proj(off_glu_b, c0)
        u_scr[u0:u0 + T, c0:c0 + MXU_COLS] = za * _sigmoid(zb)
    fill_halo(u_scr, uprev_scr, u_halo)

    def emit_conv(r, lanes, acc):
        c_scr[r:r + CONV_ROWS, lanes] = acc
    causal_conv(u_scr, cw_ref, cb_ref, emit_conv)

    for r in range(0, T, NORM_ROWS):
        cv = c_scr[r:r + NORM_ROWS, :]
        mu = jnp.mean(cv, axis=-1, keepdims=True)
        xc = cv - mu
        var = jnp.mean(xc * xc, axis=-1, keepdims=True)
        y = xc * lax.rsqrt(var + EPS) * lng_ref[...] + lnb_ref[...]
        v_scr[r:r + NORM_ROWS, :] = _silu(y).astype(jnp.bfloat16)

    for c0 in range(0, D, MXU_COLS):
        cols = slice(c0, c0 + MXU_COLS)
        ya = _dot(v_scr[...], cwo_ref[:, cols]) + cbo_ref[:, cols]
        m_scr[:, cols] = _sigmoid(in_proj(off_ga, c0)) * ya

    rx0 = SUBLANES * rx_halo
    for c0 in range(0, d_rnn, MXU_COLS):
        rx_scr[rx0:rx0 + T, c0:c0 + MXU_COLS] = in_proj(off_rx, c0)
    fill_halo(rx_scr, rxprev_scr, rx_halo)

    def emit_rx(r, lanes, acc):
        xb_scr[r:r + CONV_ROWS, lanes] = acc
        xbb_scr[r:r + CONV_ROWS, lanes] = acc.astype(jnp.bfloat16)
    causal_conv(rx_scr, lcw_ref, lcb_ref, emit_rx)

    lam = lam_ref[...]
    neg_c_sp = -LRU_C * (jnp.maximum(-lam, 0.0) + jnp.log1p(jnp.exp(-jnp.abs(lam))))

    sub = lax.broadcasted_iota(jnp.int32, (SUBLANES, MXU_COLS), 0)
    for j, ks in enumerate(gate_starts):
        c0 = j * MXU_COLS
        cols = slice(c0, c0 + MXU_COLS)
        g = _dot(xbb_scr[:, ks:ks + GATE_K], wg_ref[j])
        r_gate = _sigmoid(g[:, :MXU_COLS] + ba_ref[:, cols])
        i_gate = _sigmoid(g[:, MXU_COLS:] + bx_ref[:, cols])
        log_a = neg_c_sp[:, cols] * r_gate
        a = jnp.exp(log_a)
        mult = jnp.sqrt(-jnp.tanh(log_a) * (1.0 + a * a))
        b = mult * (i_gate * xb_scr[:, cols])

        a3 = a.reshape(L, SUBLANES, MXU_COLS)
        b3 = b.reshape(L, SUBLANES, MXU_COLS)
        hloc, prod = [b3[0]], [a3[0]]
        for i in range(1, L):
            hloc.append(a3[i] * hloc[-1] + b3[i])
            prod.append(a3[i] * prod[-1])
        ea, eb = prod[-1], hloc[-1]
        d = 1
        while d < SUBLANES:
            keep = sub >= d
            ea_sh = jnp.where(keep, pltpu.roll(ea, d, 0), 1.0)
            eb_sh = jnp.where(keep, pltpu.roll(eb, d, 0), 0.0)
            eb = eb + ea * eb_sh
            ea = ea * ea_sh
            d *= 2
        h_in = hcar_scr[:, cols]
        ends = eb + ea * h_in
        start = jnp.where(sub == 0, h_in, pltpu.roll(ends, 1, 0))
        hcar_scr[:, cols] = jnp.broadcast_to(ends[SUBLANES - 1:SUBLANES, :], (SUBLANES, MXU_COLS))
        hseq = jnp.concatenate([hl + pr * start for hl, pr in zip(hloc, prod)], axis=0)

        y_scr[:, cols] = (hseq * _gelu_tanh(in_proj(off_rg, c0))).astype(jnp.bfloat16)

    for c0 in range(0, D, MXU_COLS):
        cols = slice(c0, c0 + MXU_COLS)
        yb = _dot(y_scr[...], lwo_ref[:, cols])
        mb_scr[:, cols] = (m_scr[:, cols] + _sigmoid(in_proj(off_gb, c0)) * yb).astype(jnp.bfloat16)

    for c0 in range(0, D, MXU_COLS):
        res = _dot(mb_scr[...], wmix_ref[:, c0:c0 + MXU_COLS])
        for q in range(MXU_COLS // LANES):
            op_scr[c0 // LANES + q] = res[:, q * LANES:(q + 1) * LANES]
    for r in range(0, T, NORM_ROWS):
        for c in range(D // LANES):
            lanes = slice(c * LANES, (c + 1) * LANES)
            o_ref[r:r + NORM_ROWS, lanes] = (x_ref[r:r + NORM_ROWS, lanes]
                                             + op_scr[c, interleaved_rows(r, NORM_ROWS), :])


def _ffn_kernel(x_ref, n2g_ref, w1_ref, w3_ref, w2_ref, nfg_ref, o_ref, h_scr, act_scr, y_scr):
    tm, D = x_ref.shape
    d_ff = w1_ref.shape[1]

    for r in range(0, tm, NORM_ROWS):
        x = x_ref[r:r + NORM_ROWS, :]
        ms = jnp.mean(x * x, axis=-1, keepdims=True)
        h_scr[r:r + NORM_ROWS, :] = (x * lax.rsqrt(ms + EPS) * n2g_ref[...]).astype(jnp.bfloat16)

    for c0 in range(0, d_ff, MXU_COLS):
        cols = slice(c0, c0 + MXU_COLS)
        g = _dot(h_scr[...], w1_ref[:, cols])
        u = _dot(h_scr[...], w3_ref[:, cols])
        act_scr[:, cols] = (_silu(g) * u).astype(jnp.bfloat16)

    for c0 in range(0, D, MXU_COLS):
        cols = slice(c0, c0 + MXU_COLS)
        y_scr[:, cols] = x_ref[:, cols] + _dot(act_scr[...], w2_ref[:, cols])

    for r in range(0, tm, NORM_ROWS):
        y = y_scr[r:r + NORM_ROWS, :]
        ms = jnp.mean(y * y, axis=-1, keepdims=True)
        o_ref[r:r + NORM_ROWS, :] = y * lax.rsqrt(ms + EPS) * nfg_ref[...]


def _resident(shape):
    return pl.BlockSpec(shape, lambda *_: (0,) * len(shape), pipeline_mode=pl.Buffered(1))


def _gate_weights(wa, wx, gate_starts):
    heads, blk, _ = wa.shape
    d_rnn = heads * blk
    eye = jnp.eye(heads, dtype=wa.dtype)

    def dense(w):
        return (eye[:, None, :, None] * w[:, :, None, :]).reshape(d_rnn, d_rnn)

    da, dx = dense(wa), dense(wx)
    tiles = []
    for j, ks in enumerate(gate_starts):
        cols = slice(j * MXU_COLS, (j + 1) * MXU_COLS)
        tiles.append(jnp.concatenate([da[ks:ks + GATE_K, cols], dx[ks:ks + GATE_K, cols]], axis=1))
    return jnp.stack(tiles).astype(jnp.bfloat16)


def kernel(x, norm1_g, w_in, b_in, conv_dw_w, conv_dw_b, conv_ln_g, conv_ln_b, conv_w_out, conv_b_out,
           lru_conv_w, lru_conv_b, lru_wa, lru_ba, lru_wx, lru_bx, lru_lambda, lru_w_out, w_mix_out,
           norm2_g, ffn_w1, ffn_w3, ffn_w2, norm_f_g):
    B, S, D = x.shape
    depth = w_in.shape[0]
    d_in = w_in.shape[2]
    conv_k, d_conv = conv_dw_w.shape[1:]
    lru_k, d_rnn = lru_conv_w.shape[1:]
    heads, blk = lru_wa.shape[1:3]
    d_ff = ffn_w1.shape[2]
    T = MIX_TILE
    assert S % T == 0 and (B * S) % FFN_TILE == 0
    assert d_in == 2 * d_conv + 2 * d_rnn + 2 * D and heads * blk == d_rnn
    assert d_conv % MXU_COLS == 0 and d_rnn % MXU_COLS == 0 and D % MXU_COLS == 0 and d_ff % MXU_COLS == 0
    u_halo_rows = SUBLANES * (conv_k - 1)
    rx_halo_rows = SUBLANES * (lru_k - 1)
    gate_starts = _gate_windows(d_rnn, blk)
    n_gate = len(gate_starts)
    bf16 = jnp.bfloat16
    f32 = jnp.float32

    def row(v):
        return v.reshape(1, -1)

    for l in range(depth):
        wg = _gate_weights(lru_wa[l], lru_wx[l], gate_starts)
        mixer = pl.pallas_call(
            functools.partial(_mixer_kernel, gate_starts=gate_starts),
            grid=(B, S // T),
            in_specs=[
                pl.BlockSpec((None, T, D), lambda b, s: (b, s, 0)),
                _resident((1, D)),
                _resident((D, d_in)), _resident((1, d_in)),
                _resident((conv_k, d_conv)), _resident((1, d_conv)),
                _resident((1, d_conv)), _resident((1, d_conv)),
                _resident((d_conv, D)), _resident((1, D)),
                _resident((lru_k, d_rnn)), _resident((1, d_rnn)),
                _resident((n_gate, GATE_K, 2 * MXU_COLS)),
                _resident((1, d_rnn)), _resident((1, d_rnn)), _resident((1, d_rnn)),
                _resident((d_rnn, D)), _resident((D, D)),
            ],
            out_specs=pl.BlockSpec((None, T, D), lambda b, s: (b, s, 0)),
            out_shape=jax.ShapeDtypeStruct((B, S, D), f32),
            scratch_shapes=[
                pltpu.VMEM((D // LANES, T, LANES), f32),
                pltpu.VMEM((T, D), bf16),
                pltpu.VMEM((u_halo_rows + T, d_conv), f32),
                pltpu.VMEM((u_halo_rows, d_conv), f32),
                pltpu.VMEM((T, d_conv), f32),
                pltpu.VMEM((T, d_conv), bf16),
                pltpu.VMEM((rx_halo_rows + T, d_rnn), f32),
                pltpu.VMEM((rx_halo_rows, d_rnn), f32),
                pltpu.VMEM((T, d_rnn), f32),
                pltpu.VMEM((T, d_rnn), bf16),
                pltpu.VMEM((T, d_rnn), bf16),
                pltpu.VMEM((T, D), f32),
                pltpu.VMEM((T, D), bf16),
                pltpu.VMEM((D // LANES, T, LANES), f32),
                pltpu.VMEM((SUBLANES, d_rnn), f32),
            ],
            compiler_params=pltpu.CompilerParams(
                dimension_semantics=("arbitrary", "arbitrary"),
                vmem_limit_bytes=MIX_VMEM_BYTES),
            name="token_mixer",
        )
        x = mixer(
            x, row(norm1_g[l]), w_in[l].astype(bf16), row(b_in[l]),
            conv_dw_w[l], row(conv_dw_b[l]), row(conv_ln_g[l]), row(conv_ln_b[l]),
            conv_w_out[l].astype(bf16), row(conv_b_out[l]),
            lru_conv_w[l], row(lru_conv_b[l]), wg, row(lru_ba[l]), row(lru_bx[l]), row(lru_lambda[l]),
            lru_w_out[l].astype(bf16), w_mix_out[l].astype(bf16))

        last = l == depth - 1
        ffn = pl.pallas_call(
            _ffn_kernel,
            grid=(B * S // FFN_TILE,),
            in_specs=[
                pl.BlockSpec((FFN_TILE, D), lambda i: (i, 0)),
                _resident((1, D)),
                _resident((D, d_ff)), _resident((D, d_ff)), _resident((d_ff, D)),
                _resident((1, D)),
            ],
            out_specs=pl.BlockSpec((FFN_TILE, D), lambda i: (i, 0)),
            out_shape=jax.ShapeDtypeStruct((B * S, D), f32),
            scratch_shapes=[
                pltpu.VMEM((FFN_TILE, D), bf16),
                pltpu.VMEM((FFN_TILE, d_ff), bf16),
                pltpu.VMEM((FFN_TILE, D), f32),
            ],
            compiler_params=pltpu.CompilerParams(
                dimension_semantics=("arbitrary",),
                vmem_limit_bytes=FFN_VMEM_BYTES),
            name="channel_mixer",
        )
        assert last, "the final rmsnorm is fused into the last layer's channel mixer"
        x = ffn(x.reshape(B * S, D), row(norm2_g[l]), ffn_w1[l].astype(bf16), ffn_w3[l].astype(bf16),
                ffn_w2[l].astype(bf16), row(norm_f_g)).reshape(B, S, D)
    return x
```

```python
import functools
import math

import jax
import jax.numpy as jnp
import numpy as np
from jax import lax
from jax.experimental import pallas as pl
from jax.experimental.pallas import tpu as pltpu

EPS = 1e-6
LRU_C = 8.0

SUBLANES = 8
LANES = 128
MXU_COLS = 256

MIX_TILE = 256
FFN_TILE = 512
CONV_ROWS = 64
NORM_ROWS = 32
GATE_K = 512

MIX_VMEM_BYTES = 48 * 1024 * 1024
FFN_VMEM_BYTES = 44 * 1024 * 1024


def _sigmoid(x):
    return 0.5 * jnp.tanh(0.5 * x) + 0.5


def _silu(x):
    return x * _sigmoid(x)


def _twice_sigmoid_of_twice(zh):
    return 1.0 + jnp.tanh(zh)


def _gelu_tanh_of_twice(xh):
    c = math.sqrt(2.0 / math.pi)
    return xh * (1.0 + jnp.tanh(xh * (2.0 * c + (8.0 * 0.044715 * c) * (xh * xh))))


def _dot(a, b):
    return jnp.dot(a, b, preferred_element_type=jnp.float32)


def _gate_windows(d_rnn, block):
    starts = []
    for j in range(d_rnn // MXU_COLS):
        lo = (j * MXU_COLS // block) * block
        hi = -(-((j + 1) * MXU_COLS) // block) * block
        start = min(lo // LANES * LANES, d_rnn - GATE_K)
        assert start <= lo and hi <= start + GATE_K
        starts.append(start)
    return tuple(starts)


def _mixer_kernel(x_ref, n1g_ref, win_ref, bin_ref, cw_ref, cb_ref, lng_ref, lnb_ref, cwo_ref, cbo_ref,
                  lcw_ref, lcb_ref, wg_ref, ba_ref, bx_ref, lam_ref, lwo_ref, wmix_ref,
                  o_ref,
                  hp_scr, h_scr, u_scr, uprev_scr, c_scr, v_scr, rx_scr, rxprev_scr, xb_scr, xbb_scr,
                  grg_scr, y_scr, sga_scr, sgb_scr, m_scr, mb_scr, op_scr, hcar_scr,
                  *, gate_starts):
    T, D = x_ref.shape
    L = T // SUBLANES
    d_conv = cwo_ref.shape[0]
    d_rnn = lwo_ref.shape[0]
    u_halo = cw_ref.shape[0] - 1
    rx_halo = lcw_ref.shape[0] - 1
    assert u_halo <= L and rx_halo <= L and L % NORM_ROWS == 0 and T % CONV_ROWS == 0
    u0 = SUBLANES * u_halo
    rx0 = SUBLANES * rx_halo
    off_glu_b = d_conv
    off_rx = 2 * d_conv
    off_rg = off_rx + d_rnn
    off_ga = off_rg + d_rnn
    off_gb = off_ga + D
    f32, bf16 = jnp.float32, jnp.bfloat16

    @pl.when(pl.program_id(1) == 0)
    def _():
        uprev_scr[...] = jnp.zeros(uprev_scr.shape, f32)
        rxprev_scr[...] = jnp.zeros(rxprev_scr.shape, f32)
        hcar_scr[...] = jnp.zeros(hcar_scr.shape, f32)

    def interleaved_rows(r, n):
        s, i0 = divmod(r, L)
        return pl.ds(SUBLANES * i0 + s, n, stride=SUBLANES)

    def in_proj(off, c0):
        cols = slice(off + c0, off + c0 + MXU_COLS)
        return _dot(h_scr[...], win_ref[:, cols]) + bin_ref[:, cols]

    def fill_halo(buf, prev, halo):
        n = SUBLANES * halo
        tail = buf[T:T + n, :]
        tail3 = tail.reshape(halo, SUBLANES, tail.shape[1])
        sub = lax.broadcasted_iota(jnp.int32, tail3.shape, 1)
        mixed = jnp.where(sub == SUBLANES - 1, prev[...].reshape(tail3.shape), tail3)
        buf[0:n, :] = pltpu.roll(mixed, 1, 1).reshape(tail.shape)
        prev[...] = tail

    def conv_block(buf, buf_l0, w_ref, b_ref, l0, r):
        lanes = slice(l0, l0 + LANES)
        buf_lanes = slice(buf_l0, buf_l0 + LANES)
        acc = jnp.broadcast_to(b_ref[:, lanes], (CONV_ROWS, LANES))
        for k in range(w_ref.shape[0]):
            row = r + SUBLANES * k
            acc = acc + w_ref[k:k + 1, lanes] * buf[row:row + CONV_ROWS, buf_lanes]
        return acc

    def norm_block(r):
        x = x_ref[r:r + NORM_ROWS, :]
        ms = jnp.mean(x * x, axis=-1, keepdims=True)
        h = x * lax.rsqrt(ms + EPS) * n1g_ref[...]
        for c in range(D // LANES):
            hp_scr[c, interleaved_rows(r, NORM_ROWS), :] = h[:, c * LANES:(c + 1) * LANES]

    def norm_cast():
        for c in range(D // LANES):
            h_scr[:, c * LANES:(c + 1) * LANES] = hp_scr[c].astype(bf16)

    norm = [functools.partial(norm_block, r) for r in range(0, T, NORM_ROWS)] + [norm_cast]

    def glu_chunk(c0):
        za = in_proj(0, c0)
        zb = in_proj(off_glu_b, c0)
        u_scr[c0 // MXU_COLS, u0:u0 + T, :] = za * _twice_sigmoid_of_twice(zb)

    def rx_chunk(c0):
        rx_scr[rx0:rx0 + T, c0:c0 + MXU_COLS] = in_proj(off_rx, c0)

    def rg_chunk(c0):
        grg_scr[:, c0:c0 + MXU_COLS] = _gelu_tanh_of_twice(in_proj(off_rg, c0))

    def ga_chunk(c0):
        sga_scr[:, c0:c0 + MXU_COLS] = _twice_sigmoid_of_twice(in_proj(off_ga, c0))

    def gb_chunk(c0):
        sgb_scr[:, c0:c0 + MXU_COLS] = _twice_sigmoid_of_twice(in_proj(off_gb, c0))

    conv_chunks = list(range(0, d_conv, MXU_COLS))
    rnn_chunks = list(range(0, d_rnn, MXU_COLS))
    out_chunks = list(range(0, D, MXU_COLS))
    glu = [functools.partial(glu_chunk, c) for c in conv_chunks]
    rx = [functools.partial(rx_chunk, c) for c in rnn_chunks]
    rg = [functools.partial(rg_chunk, c) for c in rnn_chunks]
    ga = [functools.partial(ga_chunk, c) for c in out_chunks]
    gb = [functools.partial(gb_chunk, c) for c in out_chunks]

    def conv31_block(l0, r):
        c_scr[r:r + CONV_ROWS, l0:l0 + LANES] = conv_block(
            u_scr.at[l0 // MXU_COLS], l0 % MXU_COLS, cw_ref, cb_ref, l0, r)

    def conv31_items(c0):
        chunk = c0 // MXU_COLS
        items = [functools.partial(fill_halo, u_scr.at[chunk], uprev_scr.at[chunk], u_halo)]
        for l0 in range(c0, c0 + MXU_COLS, LANES):
            for r in range(0, T, CONV_ROWS):
                items.append(functools.partial(conv31_block, l0, r))
        return items

    def ln_block(r):
        cv = c_scr[r:r + NORM_ROWS, :]
        mu = jnp.mean(cv, axis=-1, keepdims=True)
        xc = cv - mu
        var = jnp.mean(xc * xc, axis=-1, keepdims=True)
        yh = xc * lax.rsqrt(var + EPS) * (0.5 * lng_ref[...]) + 0.5 * lnb_ref[...]
        v_scr[r:r + NORM_ROWS, :] = (yh * _twice_sigmoid_of_twice(yh)).astype(bf16)

    def conv4_block(l0, r):
        acc = conv_block(rx_scr, l0, lcw_ref, lcb_ref, l0, r)
        xb_scr[r:r + CONV_ROWS, l0:l0 + LANES] = acc
        xbb_scr[r:r + CONV_ROWS, l0:l0 + LANES] = acc.astype(bf16)

    conv4 = [functools.partial(fill_halo, rx_scr, rxprev_scr, rx_halo)]
    for l0 in range(0, d_rnn, LANES):
        for r in range(0, T, CONV_ROWS):
            conv4.append(functools.partial(conv4_block, l0, r))

    def gate_chunk(j, ks):
        c0 = j * MXU_COLS
        cols = slice(c0, c0 + MXU_COLS)
        lam = lam_ref[:, cols]
        half_neg_c_sp = (-0.5 * LRU_C) * (jnp.maximum(-lam, 0.0) + jnp.log1p(jnp.exp(-jnp.abs(lam))))
        sub = lax.broadcasted_iota(jnp.int32, (SUBLANES, MXU_COLS), 0)
        g = _dot(xbb_scr[:, ks:ks + GATE_K], wg_ref[j])
        r_gate2 = _twice_sigmoid_of_twice(g[:, :MXU_COLS] + 0.5 * ba_ref[:, cols])
        i_gate = 0.5 * jnp.tanh(g[:, MXU_COLS:] + 0.5 * bx_ref[:, cols]) + 0.5
        log_a = half_neg_c_sp * r_gate2
        a = jnp.exp(log_a)
        m2 = -jnp.tanh(log_a) * (1.0 + a * a)
        mult = jnp.where(m2 > 0.0, m2 * lax.rsqrt(m2), 0.0)
        b = mult * (i_gate * xb_scr[:, cols])

        a3 = a.reshape(L, SUBLANES, MXU_COLS)
        b3 = b.reshape(L, SUBLANES, MXU_COLS)
        hloc, prod = [b3[0]], [a3[0]]
        for i in range(1, L):
            hloc.append(a3[i] * hloc[-1] + b3[i])
            prod.append(a3[i] * prod[-1])
        ea, eb = prod[-1], hloc[-1]
        d = 1
        while d < SUBLANES:
            keep = sub >= d
            ea_sh = jnp.where(keep, pltpu.roll(ea, d, 0), 1.0)
            eb_sh = jnp.where(keep, pltpu.roll(eb, d, 0), 0.0)
            eb = eb + ea * eb_sh
            ea = ea * ea_sh
            d *= 2
        h_in = hcar_scr[:, cols]
        ends = eb + ea * h_in
        start = jnp.where(sub == 0, h_in, pltpu.roll(ends, 1, 0))
        hcar_scr[:, cols] = jnp.broadcast_to(ends[SUBLANES - 1:SUBLANES, :], (SUBLANES, MXU_COLS))
        hseq = jnp.concatenate([hl + pr * start for hl, pr in zip(hloc, prod)], axis=0)
        y_scr[:, cols] = (hseq * grg_scr[:, cols]).astype(bf16)

    gates = [functools.partial(gate_chunk, j, ks) for j, ks in enumerate(gate_starts)]
    conv31 = [conv31_items(c0) for c0 in conv_chunks]
    ln = [functools.partial(ln_block, r) for r in range(0, T, NORM_ROWS)]

    def lwo_chunk(c0):
        cols = slice(c0, c0 + MXU_COLS)
        m_scr[:, cols] = sgb_scr[:, cols] * _dot(y_scr[...], lwo_ref[:, cols])

    def cwo_chunk(c0):
        cols = slice(c0, c0 + MXU_COLS)
        ya = _dot(v_scr[...], cwo_ref[:, cols]) + 0.5 * cbo_ref[:, cols]
        mb_scr[:, cols] = (sga_scr[:, cols] * ya + m_scr[:, cols]).astype(bf16)

    def mix_chunk(c0):
        res = _dot(mb_scr[...], wmix_ref[:, c0:c0 + MXU_COLS])
        for q in range(MXU_COLS // LANES):
            op_scr[c0 // LANES + q] = res[:, q * LANES:(q + 1) * LANES]

    def out_block(r):
        for c in range(D // LANES):
            lanes = slice(c * LANES, (c + 1) * LANES)
            o_ref[r:r + NORM_ROWS, lanes] = (x_ref[r:r + NORM_ROWS, lanes]
                                             + op_scr[c, interleaved_rows(r, NORM_ROWS), :])

    lwo = [functools.partial(lwo_chunk, c) for c in out_chunks]
    cwo = [functools.partial(cwo_chunk, c) for c in out_chunks]
    mix = [functools.partial(mix_chunk, c) for c in out_chunks]
    out = [functools.partial(out_block, r) for r in range(0, T, NORM_ROWS)]

    assert len(glu) == 4 and len(rx) == 5 and len(ga) == 4, "the order below is written for these counts"
    order = (norm + glu[:1]
             + conv31[0] + rx + glu[1:2]
             + conv4 + conv31[1] + glu[2:] + rg[:3]
             + conv31[2] + gates[:2] + rg[3:] + ga
             + conv31[3] + gates[2:] + gb
             + ln + lwo + cwo + mix + out)
    for item in order:
        item()


def _ffn_kernel(x_ref, n2g_ref, w1_ref, w3_ref, w2_ref, nfg_ref, o_ref, h_scr, act_scr, y_scr):
    tm, D = x_ref.shape
    d_ff = w1_ref.shape[1]

    for r in range(0, tm, NORM_ROWS):
        x = x_ref[r:r + NORM_ROWS, :]
        ms = jnp.mean(x * x, axis=-1, keepdims=True)
        h_scr[r:r + NORM_ROWS, :] = (x * lax.rsqrt(ms + EPS) * n2g_ref[...]).astype(jnp.bfloat16)

    for c0 in range(0, d_ff, MXU_COLS):
        cols = slice(c0, c0 + MXU_COLS)
        g = _dot(h_scr[...], w1_ref[:, cols])
        u = _dot(h_scr[...], w3_ref[:, cols])
        act_scr[:, cols] = (_silu(g) * u).astype(jnp.bfloat16)

    for c0 in range(0, D, MXU_COLS):
        cols = slice(c0, c0 + MXU_COLS)
        y_scr[:, cols] = x_ref[:, cols] + _dot(act_scr[...], w2_ref[:, cols])

    for r in range(0, tm, NORM_ROWS):
        y = y_scr[r:r + NORM_ROWS, :]
        ms = jnp.mean(y * y, axis=-1, keepdims=True)
        o_ref[r:r + NORM_ROWS, :] = y * lax.rsqrt(ms + EPS) * nfg_ref[...]


def _resident(shape):
    return pl.BlockSpec(shape, lambda *_: (0,) * len(shape), pipeline_mode=pl.Buffered(1))


def _gate_weights(wa, wx, gate_starts):
    heads, blk, _ = wa.shape
    d_rnn = heads * blk
    head_of = np.arange(d_rnn) // blk
    same_head = jnp.asarray(head_of[:, None] == head_of[None, :])

    def dense(w):
        return jnp.where(same_head, jnp.tile(w.reshape(d_rnn, blk), (1, heads)), 0.0)

    da, dx = dense(wa), dense(wx)
    tiles = []
    for j, ks in enumerate(gate_starts):
        cols = slice(j * MXU_COLS, (j + 1) * MXU_COLS)
        tiles.append(jnp.concatenate([da[ks:ks + GATE_K, cols], dx[ks:ks + GATE_K, cols]], axis=1))
    return (0.5 * jnp.stack(tiles)).astype(jnp.bfloat16)


def kernel(x, norm1_g, w_in, b_in, conv_dw_w, conv_dw_b, conv_ln_g, conv_ln_b, conv_w_out, conv_b_out,
           lru_conv_w, lru_conv_b, lru_wa, lru_ba, lru_wx, lru_bx, lru_lambda, lru_w_out, w_mix_out,
           norm2_g, ffn_w1, ffn_w3, ffn_w2, norm_f_g):
    B, S, D = x.shape
    depth = w_in.shape[0]
    d_in = w_in.shape[2]
    conv_k, d_conv = conv_dw_w.shape[1:]
    lru_k, d_rnn = lru_conv_w.shape[1:]
    heads, blk = lru_wa.shape[1:3]
    d_ff = ffn_w1.shape[2]
    T = MIX_TILE
    assert S % T == 0 and (B * S) % FFN_TILE == 0
    assert d_in == 2 * d_conv + 2 * d_rnn + 2 * D and heads * blk == d_rnn
    assert d_conv % MXU_COLS == 0 and d_rnn % MXU_COLS == 0 and D % MXU_COLS == 0 and d_ff % MXU_COLS == 0
    u_halo_rows = SUBLANES * (conv_k - 1)
    rx_halo_rows = SUBLANES * (lru_k - 1)
    gate_starts = _gate_windows(d_rnn, blk)
    n_gate = len(gate_starts)
    bf16 = jnp.bfloat16
    f32 = jnp.float32

    def row(v):
        return v.reshape(1, -1)

    for l in range(depth):
        wg = _gate_weights(lru_wa[l], lru_wx[l], gate_starts)
        mixer = pl.pallas_call(
            functools.partial(_mixer_kernel, gate_starts=gate_starts),
            grid=(B, S // T),
            in_specs=[
                pl.BlockSpec((None, T, D), lambda b, s: (b, s, 0)),
                _resident((1, D)),
                _resident((D, d_in)), _resident((1, d_in)),
                _resident((conv_k, d_conv)), _resident((1, d_conv)),
                _resident((1, d_conv)), _resident((1, d_conv)),
                _resident((d_conv, D)), _resident((1, D)),
                _resident((lru_k, d_rnn)), _resident((1, d_rnn)),
                _resident((n_gate, GATE_K, 2 * MXU_COLS)),
                _resident((1, d_rnn)), _resident((1, d_rnn)), _resident((1, d_rnn)),
                _resident((d_rnn, D)), _resident((D, D)),
            ],
            out_specs=pl.BlockSpec((None, T, D), lambda b, s: (b, s, 0)),
            out_shape=jax.ShapeDtypeStruct((B, S, D), f32),
            scratch_shapes=[
                pltpu.VMEM((D // LANES, T, LANES), f32),
                pltpu.VMEM((T, D), bf16),
                pltpu.VMEM((d_conv // MXU_COLS, u_halo_rows + T, MXU_COLS), f32),
                pltpu.VMEM((d_conv // MXU_COLS, u_halo_rows, MXU_COLS), f32),
                pltpu.VMEM((T, d_conv), f32),
                pltpu.VMEM((T, d_conv), bf16),
                pltpu.VMEM((rx_halo_rows + T, d_rnn), f32),
                pltpu.VMEM((rx_halo_rows, d_rnn), f32),
                pltpu.VMEM((T, d_rnn), f32),
                pltpu.VMEM((T, d_rnn), bf16),
                pltpu.VMEM((T, d_rnn), f32),
                pltpu.VMEM((T, d_rnn), bf16),
                pltpu.VMEM((T, D), f32),
                pltpu.VMEM((T, D), f32),
                pltpu.VMEM((T, D), f32),
                pltpu.VMEM((T, D), bf16),
                pltpu.VMEM((D // LANES, T, LANES), f32),
                pltpu.VMEM((SUBLANES, d_rnn), f32),
            ],
            compiler_params=pltpu.CompilerParams(
                dimension_semantics=("arbitrary", "arbitrary"),
                vmem_limit_bytes=MIX_VMEM_BYTES),
            name="token_mixer",
        )
        in_scale = jnp.asarray(np.concatenate([
            np.full(2 * d_conv, 0.5), np.ones(d_rnn), np.full(d_rnn + 2 * D, 0.5)]).astype(np.float32))
        x = mixer(
            x, row(norm1_g[l]), (w_in[l] * in_scale).astype(bf16), row(b_in[l] * in_scale),
            conv_dw_w[l], row(conv_dw_b[l]), row(conv_ln_g[l]), row(conv_ln_b[l]),
            (0.5 * conv_w_out[l]).astype(bf16), row(conv_b_out[l]),
            lru_conv_w[l], row(lru_conv_b[l]), wg, row(lru_ba[l]), row(lru_bx[l]), row(lru_lambda[l]),
            (0.5 * lru_w_out[l]).astype(bf16), w_mix_out[l].astype(bf16))

        last = l == depth - 1
        ffn = pl.pallas_call(
            _ffn_kernel,
            grid=(B * S // FFN_TILE,),
            in_specs=[
                pl.BlockSpec((FFN_TILE, D), lambda i: (i, 0)),
                _resident((1, D)),
                _resident((D, d_ff)), _resident((D, d_ff)), _resident((d_ff, D)),
                _resident((1, D)),
            ],
            out_specs=pl.BlockSpec((FFN_TILE, D), lambda i: (i, 0)),
            out_shape=jax.ShapeDtypeStruct((B * S, D), f32),
            scratch_shapes=[
                pltpu.VMEM((FFN_TILE, D), bf16),
                pltpu.VMEM((FFN_TILE, d_ff), bf16),
                pltpu.VMEM((FFN_TILE, D), f32),
            ],
            compiler_params=pltpu.CompilerParams(
                dimension_semantics=("arbitrary",),
                vmem_limit_bytes=FFN_VMEM_BYTES),
            name="channel_mixer",
        )
        assert last, "the final rmsnorm is fused into the last layer's channel mixer"
        x = ffn(x.reshape(B * S, D), row(norm2_g[l]), ffn_w1[l].astype(bf16), ffn_w3[l].astype(bf16),
                ffn_w2[l].astype(bf16), row(norm_f_g)).reshape(B, S, D)
    return x
```

```python
import functools
import math

import jax
import jax.numpy as jnp
import numpy as np
from jax import lax
from jax.experimental import pallas as pl
from jax.experimental.pallas import tpu as pltpu

EPS = 1e-6
LRU_C = 8.0

SUBLANES = 8
LANES = 128
MXU_COLS = 256

MIX_TILE = 256
FFN_TILE = 512
CONV_ROWS = 64
NORM_ROWS = 32
GATE_K = 512

MIX_VMEM_BYTES = 48 * 1024 * 1024
FFN_VMEM_BYTES = 44 * 1024 * 1024


def _sigmoid(x):
    return 0.5 * jnp.tanh(0.5 * x) + 0.5


def _silu(x):
    return x * _sigmoid(x)


def _twice_sigmoid_of_twice(zh):
    return 1.0 + jnp.tanh(zh)


def _gelu_tanh_of_twice(xh):
    c = math.sqrt(2.0 / math.pi)
    return xh * (1.0 + jnp.tanh(xh * (2.0 * c + (8.0 * 0.044715 * c) * (xh * xh))))


def _dot(a, b):
    return jnp.dot(a, b, preferred_element_type=jnp.float32)


def _gate_windows(d_rnn, block):
    starts = []
    for j in range(d_rnn // MXU_COLS):
        lo = (j * MXU_COLS // block) * block
        hi = -(-((j + 1) * MXU_COLS) // block) * block
        start = min(lo // LANES * LANES, d_rnn - GATE_K)
        assert start <= lo and hi <= start + GATE_K
        starts.append(start)
    return tuple(starts)


def _mixer_kernel(x_ref, n1g_ref, win_ref, bin_ref, cw_ref, cb_ref, lng_ref, lnb_ref, cwo_ref, cbo_ref,
                  lcw_ref, lcb_ref, wg_ref, ba_ref, bx_ref, lam_ref, lwo_ref, wmix_ref,
                  o_ref,
                  hp_scr, h_scr, u_scr, uprev_scr, c_scr, v_scr, rx_scr, rxprev_scr, xb_scr, xbb_scr,
                  y_scr, m_scr, mb_scr, op_scr, hcar_scr,
                  *, gate_starts):
    T, D = x_ref.shape
    L = T // SUBLANES
    d_conv = cwo_ref.shape[0]
    d_rnn = lwo_ref.shape[0]
    u_halo = cw_ref.shape[0] - 1
    rx_halo = lcw_ref.shape[0] - 1
    assert u_halo <= L and rx_halo <= L and L % NORM_ROWS == 0 and T % CONV_ROWS == 0
    off_glu_b = d_conv
    off_rx = 2 * d_conv
    off_rg = off_rx + d_rnn
    off_ga = off_rg + d_rnn
    off_gb = off_ga + D

    @pl.when(pl.program_id(1) == 0)
    def _():
        uprev_scr[...] = jnp.zeros(uprev_scr.shape, jnp.float32)
        rxprev_scr[...] = jnp.zeros(rxprev_scr.shape, jnp.float32)
        hcar_scr[...] = jnp.zeros(hcar_scr.shape, jnp.float32)

    def interleaved_rows(r, n):
        s, i0 = divmod(r, L)
        return pl.ds(SUBLANES * i0 + s, n, stride=SUBLANES)

    for r in range(0, T, NORM_ROWS):
        x = x_ref[r:r + NORM_ROWS, :]
        ms = jnp.mean(x * x, axis=-1, keepdims=True)
        h = x * lax.rsqrt(ms + EPS) * n1g_ref[...]
        for c in range(D // LANES):
            hp_scr[c, interleaved_rows(r, NORM_ROWS), :] = h[:, c * LANES:(c + 1) * LANES]
    for c in range(D // LANES):
        h_scr[:, c * LANES:(c + 1) * LANES] = hp_scr[c].astype(jnp.bfloat16)

    def in_proj(off, c0, width=MXU_COLS):
        cols = slice(off + c0, off + c0 + width)
        return _dot(h_scr[...], win_ref[:, cols]) + bin_ref[:, cols]

    def fill_halo(buf, prev, halo):
        n, width = SUBLANES * halo, buf.shape[1]
        tail = buf[T:T + n, :].reshape(halo, SUBLANES, width)
        sub = lax.broadcasted_iota(jnp.int32, tail.shape, 1)
        mixed = jnp.where(sub == SUBLANES - 1, prev[...].reshape(tail.shape), tail)
        buf[0:n, :] = pltpu.roll(mixed, 1, 1).reshape(n, width)
        prev[...] = tail.reshape(n, width)

    def causal_conv(buf, w_ref, b_ref, emit):
        for l0 in range(0, buf.shape[1], LANES):
            lanes = slice(l0, l0 + LANES)
            for r in range(0, T, CONV_ROWS):
                acc = jnp.broadcast_to(b_ref[:, lanes], (CONV_ROWS, LANES))
                for k in range(w_ref.shape[0]):
                    row = r + SUBLANES * k
                    acc = acc + w_ref[k:k + 1, lanes] * buf[row:row + CONV_ROWS, lanes]
                emit(r, lanes, acc)

    u0 = SUBLANES * u_halo
    for c0 in range(0, d_conv, MXU_COLS):
        za = in_proj(0, c0)
        zb = in_proj(off_glu_b, c0)
        u_scr[u0:u0 + T, c0:c0 + MXU_COLS] = za * _twice_sigmoid_of_twice(zb)
    fill_halo(u_scr, uprev_scr, u_halo)

    def emit_conv(r, lanes, acc):
        c_scr[r:r + CONV_ROWS, lanes] = acc
    causal_conv(u_scr, cw_ref, cb_ref, emit_conv)

    for r in range(0, T, NORM_ROWS):
        cv = c_scr[r:r + NORM_ROWS, :]
        mu = jnp.mean(cv, axis=-1, keepdims=True)
        xc = cv - mu
        var = jnp.mean(xc * xc, axis=-1, keepdims=True)
        yh = xc * lax.rsqrt(var + EPS) * (0.5 * lng_ref[...]) + 0.5 * lnb_ref[...]
        v_scr[r:r + NORM_ROWS, :] = (yh * _twice_sigmoid_of_twice(yh)).astype(jnp.bfloat16)

    for c0 in range(0, D, MXU_COLS):
        cols = slice(c0, c0 + MXU_COLS)
        ya = _dot(v_scr[...], cwo_ref[:, cols]) + 0.5 * cbo_ref[:, cols]
        m_scr[:, cols] = _twice_sigmoid_of_twice(in_proj(off_ga, c0)) * ya

    rx0 = SUBLANES * rx_halo
    for c0 in range(0, d_rnn, MXU_COLS):
        rx_scr[rx0:rx0 + T, c0:c0 + MXU_COLS] = in_proj(off_rx, c0)
    fill_halo(rx_scr, rxprev_scr, rx_halo)

    def emit_rx(r, lanes, acc):
        xb_scr[r:r + CONV_ROWS, lanes] = acc
        xbb_scr[r:r + CONV_ROWS, lanes] = acc.astype(jnp.bfloat16)
    causal_conv(rx_scr, lcw_ref, lcb_ref, emit_rx)

    lam = lam_ref[...]
    half_neg_c_sp = (-0.5 * LRU_C) * (jnp.maximum(-lam, 0.0) + jnp.log1p(jnp.exp(-jnp.abs(lam))))

    sub = lax.broadcasted_iota(jnp.int32, (SUBLANES, MXU_COLS), 0)
    for j, ks in enumerate(gate_starts):
        c0 = j * MXU_COLS
        cols = slice(c0, c0 + MXU_COLS)
        g = _dot(xbb_scr[:, ks:ks + GATE_K], wg_ref[j])
        r_gate2 = _twice_sigmoid_of_twice(g[:, :MXU_COLS] + 0.5 * ba_ref[:, cols])
        i_gate = 0.5 * jnp.tanh(g[:, MXU_COLS:] + 0.5 * bx_ref[:, cols]) + 0.5
        log_a = half_neg_c_sp[:, cols] * r_gate2
        a = jnp.exp(log_a)
        m2 = -jnp.tanh(log_a) * (1.0 + a * a)
        mult = jnp.where(m2 > 0.0, m2 * lax.rsqrt(m2), 0.0)
        b = mult * (i_gate * xb_scr[:, cols])

        a3 = a.reshape(L, SUBLANES, MXU_COLS)
        b3 = b.reshape(L, SUBLANES, MXU_COLS)
        hloc, prod = [b3[0]], [a3[0]]
        for i in range(1, L):
            hloc.append(a3[i] * hloc[-1] + b3[i])
            prod.append(a3[i] * prod[-1])
        ea, eb = prod[-1], hloc[-1]
        d = 1
        while d < SUBLANES:
            keep = sub >= d
            ea_sh = jnp.where(keep, pltpu.roll(ea, d, 0), 1.0)
            eb_sh = jnp.where(keep, pltpu.roll(eb, d, 0), 0.0)
            eb = eb + ea * eb_sh
            ea = ea * ea_sh
            d *= 2
        h_in = hcar_scr[:, cols]
        ends = eb + ea * h_in
        start = jnp.where(sub == 0, h_in, pltpu.roll(ends, 1, 0))
        hcar_scr[:, cols] = jnp.broadcast_to(ends[SUBLANES - 1:SUBLANES, :], (SUBLANES, MXU_COLS))
        hseq = jnp.concatenate([hl + pr * start for hl, pr in zip(hloc, prod)], axis=0)

        y_scr[:, cols] = (hseq * _gelu_tanh_of_twice(in_proj(off_rg, c0))).astype(jnp.bfloat16)

    for c0 in range(0, D, MXU_COLS):
        cols = slice(c0, c0 + MXU_COLS)
        yb = _dot(y_scr[...], lwo_ref[:, cols])
        mb_scr[:, cols] = (m_scr[:, cols] + _twice_sigmoid_of_twice(in_proj(off_gb, c0)) * yb
                           ).astype(jnp.bfloat16)

    for c0 in range(0, D, MXU_COLS):
        res = _dot(mb_scr[...], wmix_ref[:, c0:c0 + MXU_COLS])
        for q in range(MXU_COLS // LANES):
            op_scr[c0 // LANES + q] = res[:, q * LANES:(q + 1) * LANES]
    for r in range(0, T, NORM_ROWS):
        for c in range(D // LANES):
            lanes = slice(c * LANES, (c + 1) * LANES)
            o_ref[r:r + NORM_ROWS, lanes] = (x_ref[r:r + NORM_ROWS, lanes]
                                             + op_scr[c, interleaved_rows(r, NORM_ROWS), :])


def _ffn_kernel(x_ref, n2g_ref, w1_ref, w3_ref, w2_ref, nfg_ref, o_ref, h_scr, act_scr, y_scr):
    tm, D = x_ref.shape
    d_ff = w1_ref.shape[1]

    for r in range(0, tm, NORM_ROWS):
        x = x_ref[r:r + NORM_ROWS, :]
        ms = jnp.mean(x * x, axis=-1, keepdims=True)
        h_scr[r:r + NORM_ROWS, :] = (x * lax.rsqrt(ms + EPS) * n2g_ref[...]).astype(jnp.bfloat16)

    for c0 in range(0, d_ff, MXU_COLS):
        cols = slice(c0, c0 + MXU_COLS)
        g = _dot(h_scr[...], w1_ref[:, cols])
        u = _dot(h_scr[...], w3_ref[:, cols])
        act_scr[:, cols] = (_silu(g) * u).astype(jnp.bfloat16)

    for c0 in range(0, D, MXU_COLS):
        cols = slice(c0, c0 + MXU_COLS)
        y_scr[:, cols] = x_ref[:, cols] + _dot(act_scr[...], w2_ref[:, cols])

    for r in range(0, tm, NORM_ROWS):
        y = y_scr[r:r + NORM_ROWS, :]
        ms = jnp.mean(y * y, axis=-1, keepdims=True)
        o_ref[r:r + NORM_ROWS, :] = y * lax.rsqrt(ms + EPS) * nfg_ref[...]


def _resident(shape):
    return pl.BlockSpec(shape, lambda *_: (0,) * len(shape), pipeline_mode=pl.Buffered(1))


def _gate_weights(wa, wx, gate_starts):
    heads, blk, _ = wa.shape
    d_rnn = heads * blk
    head_of = np.arange(d_rnn) // blk
    same_head = jnp.asarray(head_of[:, None] == head_of[None, :])

    def dense(w):
        return jnp.where(same_head, jnp.tile(w.reshape(d_rnn, blk), (1, heads)), 0.0)

    da, dx = dense(wa), dense(wx)
    tiles = []
    for j, ks in enumerate(gate_starts):
        cols = slice(j * MXU_COLS, (j + 1) * MXU_COLS)
        tiles.append(jnp.concatenate([da[ks:ks + GATE_K, cols], dx[ks:ks + GATE_K, cols]], axis=1))
    return (0.5 * jnp.stack(tiles)).astype(jnp.bfloat16)


def kernel(x, norm1_g, w_in, b_in, conv_dw_w, conv_dw_b, conv_ln_g, conv_ln_b, conv_w_out, conv_b_out,
           lru_conv_w, lru_conv_b, lru_wa, lru_ba, lru_wx, lru_bx, lru_lambda, lru_w_out, w_mix_out,
           norm2_g, ffn_w1, ffn_w3, ffn_w2, norm_f_g):
    B, S, D = x.shape
    depth = w_in.shape[0]
    d_in = w_in.shape[2]
    conv_k, d_conv = conv_dw_w.shape[1:]
    lru_k, d_rnn = lru_conv_w.shape[1:]
    heads, blk = lru_wa.shape[1:3]
    d_ff = ffn_w1.shape[2]
    T = MIX_TILE
    assert S % T == 0 and (B * S) % FFN_TILE == 0
    assert d_in == 2 * d_conv + 2 * d_rnn + 2 * D and heads * blk == d_rnn
    assert d_conv % MXU_COLS == 0 and d_rnn % MXU_COLS == 0 and D % MXU_COLS == 0 and d_ff % MXU_COLS == 0
    u_halo_rows = SUBLANES * (conv_k - 1)
    rx_halo_rows = SUBLANES * (lru_k - 1)
    gate_starts = _gate_windows(d_rnn, blk)
    n_gate = len(gate_starts)
    bf16 = jnp.bfloat16
    f32 = jnp.float32

    def row(v):
        return v.reshape(1, -1)

    for l in range(depth):
        wg = _gate_weights(lru_wa[l], lru_wx[l], gate_starts)
        mixer = pl.pallas_call(
            functools.partial(_mixer_kernel, gate_starts=gate_starts),
            grid=(B, S // T),
            in_specs=[
                pl.BlockSpec((None, T, D), lambda b, s: (b, s, 0)),
                _resident((1, D)),
                _resident((D, d_in)), _resident((1, d_in)),
                _resident((conv_k, d_conv)), _resident((1, d_conv)),
                _resident((1, d_conv)), _resident((1, d_conv)),
                _resident((d_conv, D)), _resident((1, D)),
                _resident((lru_k, d_rnn)), _resident((1, d_rnn)),
                _resident((n_gate, GATE_K, 2 * MXU_COLS)),
                _resident((1, d_rnn)), _resident((1, d_rnn)), _resident((1, d_rnn)),
                _resident((d_rnn, D)), _resident((D, D)),
            ],
            out_specs=pl.BlockSpec((None, T, D), lambda b, s: (b, s, 0)),
            out_shape=jax.ShapeDtypeStruct((B, S, D), f32),
            scratch_shapes=[
                pltpu.VMEM((D // LANES, T, LANES), f32),
                pltpu.VMEM((T, D), bf16),
                pltpu.VMEM((u_halo_rows + T, d_conv), f32),
                pltpu.VMEM((u_halo_rows, d_conv), f32),
                pltpu.VMEM((T, d_conv), f32),
                pltpu.VMEM((T, d_conv), bf16),
                pltpu.VMEM((rx_halo_rows + T, d_rnn), f32),
                pltpu.VMEM((rx_halo_rows, d_rnn), f32),
                pltpu.VMEM((T, d_rnn), f32),
                pltpu.VMEM((T, d_rnn), bf16),
                pltpu.VMEM((T, d_rnn), bf16),
                pltpu.VMEM((T, D), f32),
                pltpu.VMEM((T, D), bf16),
                pltpu.VMEM((D // LANES, T, LANES), f32),
                pltpu.VMEM((SUBLANES, d_rnn), f32),
            ],
            compiler_params=pltpu.CompilerParams(
                dimension_semantics=("arbitrary", "arbitrary"),
                vmem_limit_bytes=MIX_VMEM_BYTES),
            name="token_mixer",
        )
        in_scale = jnp.asarray(np.concatenate([
            np.full(2 * d_conv, 0.5), np.ones(d_rnn), np.full(d_rnn + 2 * D, 0.5)]).astype(np.float32))
        x = mixer(
            x, row(norm1_g[l]), (w_in[l] * in_scale).astype(bf16), row(b_in[l] * in_scale),
            conv_dw_w[l], row(conv_dw_b[l]), row(conv_ln_g[l]), row(conv_ln_b[l]),
            (0.5 * conv_w_out[l]).astype(bf16), row(conv_b_out[l]),
            lru_conv_w[l], row(lru_conv_b[l]), wg, row(lru_ba[l]), row(lru_bx[l]), row(lru_lambda[l]),
            (0.5 * lru_w_out[l]).astype(bf16), w_mix_out[l].astype(bf16))

        last = l == depth - 1
        ffn = pl.pallas_call(
            _ffn_kernel,
            grid=(B * S // FFN_TILE,),
            in_specs=[
                pl.BlockSpec((FFN_TILE, D), lambda i: (i, 0)),
                _resident((1, D)),
                _resident((D, d_ff)), _resident((D, d_ff)), _resident((d_ff, D)),
                _resident((1, D)),
            ],
            out_specs=pl.BlockSpec((FFN_TILE, D), lambda i: (i, 0)),
            out_shape=jax.ShapeDtypeStruct((B * S, D), f32),
            scratch_shapes=[
                pltpu.VMEM((FFN_TILE, D), bf16),
                pltpu.VMEM((FFN_TILE, d_ff), bf16),
                pltpu.VMEM((FFN_TILE, D), f32),
            ],
            compiler_params=pltpu.CompilerParams(
                dimension_semantics=("arbitrary",),
                vmem_limit_bytes=FFN_VMEM_BYTES),
            name="channel_mixer",
        )
        assert last, "the final rmsnorm is fused into the last layer's channel mixer"
        x = ffn(x.reshape(B * S, D), row(norm2_g[l]), ffn_w1[l].astype(bf16), ffn_w3[l].astype(bf16),
                ffn_w2[l].astype(bf16), row(norm_f_g)).reshape(B, S, D)
    return x
```

```python
import functools
import math

import jax
import jax.numpy as jnp
import numpy as np
from jax import lax
from jax.experimental import pallas as pl
from jax.experimental.pallas import tpu as pltpu

EPS = 1e-6
LRU_C = 8.0

SUBLANES = 8
LANES = 128
MXU_COLS = 256

MIX_TILE = 256
FFN_TILE = 512
CONV_ROWS = 64
NORM_ROWS = 32
GATE_K = 512

MIX_VMEM_BYTES = 48 * 1024 * 1024
FFN_VMEM_BYTES = 44 * 1024 * 1024


def _sigmoid(x):
    return 0.5 * jnp.tanh(0.5 * x) + 0.5


def _silu(x):
    return x * _sigmoid(x)


def _gelu_tanh(x):
    c = math.sqrt(2.0 / math.pi)
    return 0.5 * x * (1.0 + jnp.tanh(c * (x + 0.044715 * (x * x * x))))


def _dot(a, b):
    return jnp.dot(a, b, preferred_element_type=jnp.float32)


def _gate_windows(d_rnn, block):
    starts = []
    for j in range(d_rnn // MXU_COLS):
        lo = (j * MXU_COLS // block) * block
        hi = -(-((j + 1) * MXU_COLS) // block) * block
        start = min(lo // LANES * LANES, d_rnn - GATE_K)
        assert start <= lo and hi <= start + GATE_K
        starts.append(start)
    return tuple(starts)


def _mixer_kernel(x_ref, n1g_ref, win_ref, bin_ref, cw_ref, cb_ref, lng_ref, lnb_ref, cwo_ref, cbo_ref,
                  lcw_ref, lcb_ref, wg_ref, ba_ref, bx_ref, lam_ref, lwo_ref, wmix_ref,
                  o_ref,
                  hp_scr, h_scr, u_scr, uprev_scr, c_scr, v_scr, rx_scr, rxprev_scr, xb_scr, xbb_scr,
                  y_scr, m_scr, mb_scr, op_scr, hcar_scr,
                  *, gate_starts):
    T, D = x_ref.shape
    L = T // SUBLANES
    d_conv = cwo_ref.shape[0]
    d_rnn = lwo_ref.shape[0]
    u_halo = cw_ref.shape[0] - 1
    rx_halo = lcw_ref.shape[0] - 1
    assert u_halo <= L and rx_halo <= L and L % NORM_ROWS == 0 and T % CONV_ROWS == 0
    off_glu_b = d_conv
    off_rx = 2 * d_conv
    off_rg = off_rx + d_rnn
    off_ga = off_rg + d_rnn
    off_gb = off_ga + D

    @pl.when(pl.program_id(1) == 0)
    def _():
        uprev_scr[...] = jnp.zeros(uprev_scr.shape, jnp.float32)
        rxprev_scr[...] = jnp.zeros(rxprev_scr.shape, jnp.float32)
        hcar_scr[...] = jnp.zeros(hcar_scr.shape, jnp.float32)

    def interleaved_rows(r, n):
        s, i0 = divmod(r, L)
        return pl.ds(SUBLANES * i0 + s, n, stride=SUBLANES)

    for r in range(0, T, NORM_ROWS):
        x = x_ref[r:r + NORM_ROWS, :]
        ms = jnp.mean(x * x, axis=-1, keepdims=True)
        h = x * lax.rsqrt(ms + EPS) * n1g_ref[...]
        for c in range(D // LANES):
            hp_scr[c, interleaved_rows(r, NORM_ROWS), :] = h[:, c * LANES:(c + 1) * LANES]
    for c in range(D // LANES):
        h_scr[:, c * LANES:(c + 1) * LANES] = hp_scr[c].astype(jnp.bfloat16)

    def in_proj(off, c0, width=MXU_COLS):
        cols = slice(off + c0, off + c0 + width)
        return _dot(h_scr[...], win_ref[:, cols]) + bin_ref[:, cols]

    def fill_halo(buf, prev, halo):
        n, width = SUBLANES * halo, buf.shape[1]
        tail = buf[T:T + n, :].reshape(halo, SUBLANES, width)
        sub = lax.broadcasted_iota(jnp.int32, tail.shape, 1)
        mixed = jnp.where(sub == SUBLANES - 1, prev[...].reshape(tail.shape), tail)
        buf[0:n, :] = pltpu.roll(mixed, 1, 1).reshape(n, width)
        prev[...] = tail.reshape(n, width)

    def causal_conv(buf, w_ref, b_ref, emit):
        for l0 in range(0, buf.shape[1], LANES):
            lanes = slice(l0, l0 + LANES)
            for r in range(0, T, CONV_ROWS):
                acc = jnp.broadcast_to(b_ref[:, lanes], (CONV_ROWS, LANES))
                for k in range(w_ref.shape[0]):
                    row = r + SUBLANES * k
                    acc = acc + w_ref[k:k + 1, lanes] * buf[row:row + CONV_ROWS, lanes]
                emit(r, lanes, acc)

    u0 = SUBLANES * u_halo
    for c0 in range(0, d_conv, MXU_COLS):
        za = in_proj(0, c0)
        zb = in_proj(off_glu_b, c0)
        u_scr[u0:u0 + T, c0:c0 + MXU_COLS] = za * _sigmoid(zb)
    fill_halo(u_scr, uprev_scr, u_halo)

    def emit_conv(r, lanes, acc):
        c_scr[r:r + CONV_ROWS, lanes] = acc
    causal_conv(u_scr, cw_ref, cb_ref, emit_conv)

    for r in range(0, T, NORM_ROWS):
        cv = c_scr[r:r + NORM_ROWS, :]
        mu = jnp.mean(cv, axis=-1, keepdims=True)
        xc = cv - mu
        var = jnp.mean(xc * xc, axis=-1, keepdims=True)
        y = xc * lax.rsqrt(var + EPS) * lng_ref[...] + lnb_ref[...]
        v_scr[r:r + NORM_ROWS, :] = _silu(y).astype(jnp.bfloat16)

    for c0 in range(0, D, MXU_COLS):
        cols = slice(c0, c0 + MXU_COLS)
        ya = _dot(v_scr[...], cwo_ref[:, cols]) + cbo_ref[:, cols]
        m_scr[:, cols] = _sigmoid(in_proj(off_ga, c0)) * ya

    rx0 = SUBLANES * rx_halo
    for c0 in range(0, d_rnn, MXU_COLS):
        rx_scr[rx0:rx0 + T, c0:c0 + MXU_COLS] = in_proj(off_rx, c0)
    fill_halo(rx_scr, rxprev_scr, rx_halo)

    def emit_rx(r, lanes, acc):
        xb_scr[r:r + CONV_ROWS, lanes] = acc
        xbb_scr[r:r + CONV_ROWS, lanes] = acc.astype(jnp.bfloat16)
    causal_conv(rx_scr, lcw_ref, lcb_ref, emit_rx)

    lam = lam_ref[...]
    neg_c_sp = -LRU_C * (jnp.maximum(-lam, 0.0) + jnp.log1p(jnp.exp(-jnp.abs(lam))))

    sub = lax.broadcasted_iota(jnp.int32, (SUBLANES, MXU_COLS), 0)
    for j, ks in enumerate(gate_starts):
        c0 = j * MXU_COLS
        cols = slice(c0, c0 + MXU_COLS)
        g = _dot(xbb_scr[:, ks:ks + GATE_K], wg_ref[j])
        r_gate = _sigmoid(g[:, :MXU_COLS] + ba_ref[:, cols])
        i_gate = _sigmoid(g[:, MXU_COLS:] + bx_ref[:, cols])
        log_a = neg_c_sp[:, cols] * r_gate
        a = jnp.exp(log_a)
        mult = jnp.sqrt(-jnp.tanh(log_a) * (1.0 + a * a))
        b = mult * (i_gate * xb_scr[:, cols])

        a3 = a.reshape(L, SUBLANES, MXU_COLS)
        b3 = b.reshape(L, SUBLANES, MXU_COLS)
        hloc, prod = [b3[0]], [a3[0]]
        for i in range(1, L):
            hloc.append(a3[i] * hloc[-1] + b3[i])
            prod.append(a3[i] * prod[-1])
        ea, eb = prod[-1], hloc[-1]
        d = 1
        while d < SUBLANES:
            keep = sub >= d
            ea_sh = jnp.where(keep, pltpu.roll(ea, d, 0), 1.0)
            eb_sh = jnp.where(keep, pltpu.roll(eb, d, 0), 0.0)
            eb = eb + ea * eb_sh
            ea = ea * ea_sh
            d *= 2
        h_in = hcar_scr[:, cols]
        ends = eb + ea * h_in
        start = jnp.where(sub == 0, h_in, pltpu.roll(ends, 1, 0))
        hcar_scr[:, cols] = jnp.broadcast_to(ends[SUBLANES - 1:SUBLANES, :], (SUBLANES, MXU_COLS))
        hseq = jnp.concatenate([hl + pr * start for hl, pr in zip(hloc, prod)], axis=0)

        y_scr[:, cols] = (hseq * _gelu_tanh(in_proj(off_rg, c0))).astype(jnp.bfloat16)

    for c0 in range(0, D, MXU_COLS):
        cols = slice(c0, c0 + MXU_COLS)
        yb = _dot(y_scr[...], lwo_ref[:, cols])
        mb_scr[:, cols] = (m_scr[:, cols] + _sigmoid(in_proj(off_gb, c0)) * yb).astype(jnp.bfloat16)

    for c0 in range(0, D, MXU_COLS):
        res = _dot(mb_scr[...], wmix_ref[:, c0:c0 + MXU_COLS])
        for q in range(MXU_COLS // LANES):
            op_scr[c0 // LANES + q] = res[:, q * LANES:(q + 1) * LANES]
    for r in range(0, T, NORM_ROWS):
        for c in range(D // LANES):
            lanes = slice(c * LANES, (c + 1) * LANES)
            o_ref[r:r + NORM_ROWS, lanes] = (x_ref[r:r + NORM_ROWS, lanes]
                                             + op_scr[c, interleaved_rows(r, NORM_ROWS), :])


def _ffn_kernel(x_ref, n2g_ref, w1_ref, w3_ref, w2_ref, nfg_ref, o_ref, h_scr, act_scr, y_scr):
    tm, D = x_ref.shape
    d_ff = w1_ref.shape[1]

    for r in range(0, tm, NORM_ROWS):
        x = x_ref[r:r + NORM_ROWS, :]
        ms = jnp.mean(x * x, axis=-1, keepdims=True)
        h_scr[r:r + NORM_ROWS, :] = (x * lax.rsqrt(ms + EPS) * n2g_ref[...]).astype(jnp.bfloat16)

    for c0 in range(0, d_ff, MXU_COLS):
        cols = slice(c0, c0 + MXU_COLS)
        g = _dot(h_scr[...], w1_ref[:, cols])
        u = _dot(h_scr[...], w3_ref[:, cols])
        act_scr[:, cols] = (_silu(g) * u).astype(jnp.bfloat16)

    for c0 in range(0, D, MXU_COLS):
        cols = slice(c0, c0 + MXU_COLS)
        y_scr[:, cols] = x_ref[:, cols] + _dot(act_scr[...], w2_ref[:, cols])

    for r in range(0, tm, NORM_ROWS):
        y = y_scr[r:r + NORM_ROWS, :]
        ms = jnp.mean(y * y, axis=-1, keepdims=True)
        o_ref[r:r + NORM_ROWS, :] = y * lax.rsqrt(ms + EPS) * nfg_ref[...]


def _resident(shape):
    return pl.BlockSpec(shape, lambda *_: (0,) * len(shape), pipeline_mode=pl.Buffered(1))


def _gate_weights(wa, wx, gate_starts):
    heads, blk, _ = wa.shape
    d_rnn = heads * blk
    head_of = np.arange(d_rnn) // blk
    same_head = jnp.asarray(head_of[:, None] == head_of[None, :])

    def dense(w):
        return jnp.where(same_head, jnp.tile(w.reshape(d_rnn, blk), (1, heads)), 0.0)

    da, dx = dense(wa), dense(wx)
    tiles = []
    for j, ks in enumerate(gate_starts):
        cols = slice(j * MXU_COLS, (j + 1) * MXU_COLS)
        tiles.append(jnp.concatenate([da[ks:ks + GATE_K, cols], dx[ks:ks + GATE_K, cols]], axis=1))
    return jnp.stack(tiles).astype(jnp.bfloat16)


def kernel(x, norm1_g, w_in, b_in, conv_dw_w, conv_dw_b, conv_ln_g, conv_ln_b, conv_w_out, conv_b_out,
           lru_conv_w, lru_conv_b, lru_wa, lru_ba, lru_wx, lru_bx, lru_lambda, lru_w_out, w_mix_out,
           norm2_g, ffn_w1, ffn_w3, ffn_w2, norm_f_g):
    B, S, D = x.shape
    depth = w_in.shape[0]
    d_in = w_in.shape[2]
    conv_k, d_conv = conv_dw_w.shape[1:]
    lru_k, d_rnn = lru_conv_w.shape[1:]
    heads, blk = lru_wa.shape[1:3]
    d_ff = ffn_w1.shape[2]
    T = MIX_TILE
    assert S % T == 0 and (B * S) % FFN_TILE == 0
    assert d_in == 2 * d_conv + 2 * d_rnn + 2 * D and heads * blk == d_rnn
    assert d_conv % MXU_COLS == 0 and d_rnn % MXU_COLS == 0 and D % MXU_COLS == 0 and d_ff % MXU_COLS == 0
    u_halo_rows = SUBLANES * (conv_k - 1)
    rx_halo_rows = SUBLANES * (lru_k - 1)
    gate_starts = _gate_windows(d_rnn, blk)
    n_gate = len(gate_starts)
    bf16 = jnp.bfloat16
    f32 = jnp.float32

    def row(v):
        return v.reshape(1, -1)

    for l in range(depth):
        wg = _gate_weights(lru_wa[l], lru_wx[l], gate_starts)
        mixer = pl.pallas_call(
            functools.partial(_mixer_kernel, gate_starts=gate_starts),
            grid=(B, S // T),
            in_specs=[
                pl.BlockSpec((None, T, D), lambda b, s: (b, s, 0)),
                _resident((1, D)),
                _resident((D, d_in)), _resident((1, d_in)),
                _resident((conv_k, d_conv)), _resident((1, d_conv)),
                _resident((1, d_conv)), _resident((1, d_conv)),
                _resident((d_conv, D)), _resident((1, D)),
                _resident((lru_k, d_rnn)), _resident((1, d_rnn)),
                _resident((n_gate, GATE_K, 2 * MXU_COLS)),
                _resident((1, d_rnn)), _resident((1, d_rnn)), _resident((1, d_rnn)),
                _resident((d_rnn, D)), _resident((D, D)),
            ],
            out_specs=pl.BlockSpec((None, T, D), lambda b, s: (b, s, 0)),
            out_shape=jax.ShapeDtypeStruct((B, S, D), f32),
            scratch_shapes=[
                pltpu.VMEM((D // LANES, T, LANES), f32),
                pltpu.VMEM((T, D), bf16),
                pltpu.VMEM((u_halo_rows + T, d_conv), f32),
                pltpu.VMEM((u_halo_rows, d_conv), f32),
                pltpu.VMEM((T, d_conv), f32),
                pltpu.VMEM((T, d_conv), bf16),
                pltpu.VMEM((rx_halo_rows + T, d_rnn), f32),
                pltpu.VMEM((rx_halo_rows, d_rnn), f32),
                pltpu.VMEM((T, d_rnn), f32),
                pltpu.VMEM((T, d_rnn), bf16),
                pltpu.VMEM((T, d_rnn), bf16),
                pltpu.VMEM((T, D), f32),
                pltpu.VMEM((T, D), bf16),
                pltpu.VMEM((D // LANES, T, LANES), f32),
                pltpu.VMEM((SUBLANES, d_rnn), f32),
            ],
            compiler_params=pltpu.CompilerParams(
                dimension_semantics=("arbitrary", "arbitrary"),
                vmem_limit_bytes=MIX_VMEM_BYTES),
            name="token_mixer",
        )
        x = mixer(
            x, row(norm1_g[l]), w_in[l].astype(bf16), row(b_in[l]),
            conv_dw_w[l], row(conv_dw_b[l]), row(conv_ln_g[l]), row(conv_ln_b[l]),
            conv_w_out[l].astype(bf16), row(conv_b_out[l]),
            lru_conv_w[l], row(lru_conv_b[l]), wg, row(lru_ba[l]), row(lru_bx[l]), row(lru_lambda[l]),
            lru_w_out[l].astype(bf16), w_mix_out[l].astype(bf16))

        last = l == depth - 1
        ffn = pl.pallas_call(
            _ffn_kernel,
            grid=(B * S // FFN_TILE,),
            in_specs=[
                pl.BlockSpec((FFN_TILE, D), lambda i: (i, 0)),
                _resident((1, D)),
                _resident((D, d_ff)), _resident((D, d_ff)), _resident((d_ff, D)),
                _resident((1, D)),
            ],
            out_specs=pl.BlockSpec((FFN_TILE, D), lambda i: (i, 0)),
            out_shape=jax.ShapeDtypeStruct((B * S, D), f32),
            scratch_shapes=[
                pltpu.VMEM((FFN_TILE, D), bf16),
                pltpu.VMEM((FFN_TILE, d_ff), bf16),
                pltpu.VMEM((FFN_TILE, D), f32),
            ],
            compiler_params=pltpu.CompilerParams(
                dimension_semantics=("arbitrary",),
                vmem_limit_bytes=FFN_VMEM_BYTES),
            name="channel_mixer",
        )
        assert last, "the final rmsnorm is fused into the last layer's channel mixer"
        x = ffn(x.reshape(B * S, D), row(norm2_g[l]), ffn_w1[l].astype(bf16), ffn_w3[l].astype(bf16),
                ffn_w2[l].astype(bf16), row(norm_f_g)).reshape(B, S, D)
    return x
```

```python
import functools
import math

import jax
import jax.numpy as jnp
import numpy as np
from jax import lax
from jax.experimental import pallas as pl
from jax.experimental.pallas import tpu as pltpu

EPS = 1e-6
LRU_C = 8.0

SUBLANES = 8
LANES = 128
MXU_COLS = 256

MIX_TILE = 256
FFN_TILE = 512
CONV_ROWS = 64
NORM_ROWS = 32
GATE_K = 512

MIX_VMEM_BYTES = 48 * 1024 * 1024
FFN_VMEM_BYTES = 44 * 1024 * 1024


def _sigmoid(x):
    return 0.5 * jnp.tanh(0.5 * x) + 0.5


def _silu(x):
    return x * _sigmoid(x)


def _gelu_tanh_of_twice(xh):
    c = math.sqrt(2.0 / math.pi)
    return xh * (1.0 + jnp.tanh(xh * (2.0 * c + (8.0 * 0.044715 * c) * (xh * xh))))


def _dot(a, b):
    return jnp.dot(a, b, preferred_element_type=jnp.float32)


def _gate_windows(d_rnn, block):
    starts = []
    for j in range(d_rnn // MXU_COLS):
        lo = (j * MXU_COLS // block) * block
        hi = -(-((j + 1) * MXU_COLS) // block) * block
        start = min(lo // LANES * LANES, d_rnn - GATE_K)
        assert start <= lo and hi <= start + GATE_K
        starts.append(start)
    return tuple(starts)


def _mixer_kernel(x_ref, n1g_ref, win_ref, bin_ref, cw_ref, cb_ref, lng_ref, lnb_ref, cwo_ref, cbo_ref,
                  lcw_ref, lcb_ref, wg_ref, ba_ref, bx_ref, lam_ref, lwo_ref, wmix_ref,
                  o_ref,
                  hp_scr, h_scr, u_scr, uprev_scr, c_scr, v_scr, rx_scr, rxprev_scr, xb_scr, xbb_scr,
                  y_scr, m_scr, mb_scr, op_scr, hcar_scr,
                  *, gate_starts):
    T, D = x_ref.shape
    L = T // SUBLANES
    d_conv = cwo_ref.shape[0]
    d_rnn = lwo_ref.shape[0]
    u_halo = cw_ref.shape[0] - 1
    rx_halo = lcw_ref.shape[0] - 1
    assert u_halo <= L and rx_halo <= L and L % NORM_ROWS == 0 and T % CONV_ROWS == 0
    off_glu_b = d_conv
    off_rx = 2 * d_conv
    off_rg = off_rx + d_rnn
    off_ga = off_rg + d_rnn
    off_gb = off_ga + D

    @pl.when(pl.program_id(1) == 0)
    def _():
        uprev_scr[...] = jnp.zeros(uprev_scr.shape, jnp.float32)
        rxprev_scr[...] = jnp.zeros(rxprev_scr.shape, jnp.float32)
        hcar_scr[...] = jnp.zeros(hcar_scr.shape, jnp.float32)

    def interleaved_rows(r, n):
        s, i0 = divmod(r, L)
        return pl.ds(SUBLANES * i0 + s, n, stride=SUBLANES)

    for r in range(0, T, NORM_ROWS):
        x = x_ref[r:r + NORM_ROWS, :]
        ms = jnp.mean(x * x, axis=-1, keepdims=True)
        h = x * lax.rsqrt(ms + EPS) * n1g_ref[...]
        for c in range(D // LANES):
            hp_scr[c, interleaved_rows(r, NORM_ROWS), :] = h[:, c * LANES:(c + 1) * LANES]
    for c in range(D // LANES):
        h_scr[:, c * LANES:(c + 1) * LANES] = hp_scr[c].astype(jnp.bfloat16)

    def in_proj(off, c0, width=MXU_COLS):
        cols = slice(off + c0, off + c0 + width)
        return _dot(h_scr[...], win_ref[:, cols]) + bin_ref[:, cols]

    def fill_halo(buf, prev, halo):
        n, width = SUBLANES * halo, buf.shape[1]
        tail = buf[T:T + n, :].reshape(halo, SUBLANES, width)
        sub = lax.broadcasted_iota(jnp.int32, tail.shape, 1)
        mixed = jnp.where(sub == SUBLANES - 1, prev[...].reshape(tail.shape), tail)
        buf[0:n, :] = pltpu.roll(mixed, 1, 1).reshape(n, width)
        prev[...] = tail.reshape(n, width)

    def causal_conv(buf, w_ref, b_ref, emit):
        for l0 in range(0, buf.shape[1], LANES):
            lanes = slice(l0, l0 + LANES)
            for r in range(0, T, CONV_ROWS):
                acc = jnp.broadcast_to(b_ref[:, lanes], (CONV_ROWS, LANES))
                for k in range(w_ref.shape[0]):
                    row = r + SUBLANES * k
                    acc = acc + w_ref[k:k + 1, lanes] * buf[row:row + CONV_ROWS, lanes]
                emit(r, lanes, acc)

    u0 = SUBLANES * u_halo
    for c0 in range(0, d_conv, MXU_COLS):
        za = in_proj(0, c0)
        zb = in_proj(off_glu_b, c0)
        u_scr[u0:u0 + T, c0:c0 + MXU_COLS] = za * _sigmoid(zb)
    fill_halo(u_scr, uprev_scr, u_halo)

    def emit_conv(r, lanes, acc):
        c_scr[r:r + CONV_ROWS, lanes] = acc
    causal_conv(u_scr, cw_ref, cb_ref, emit_conv)

    for r in range(0, T, NORM_ROWS):
        cv = c_scr[r:r + NORM_ROWS, :]
        mu = jnp.mean(cv, axis=-1, keepdims=True)
        xc = cv - mu
        var = jnp.mean(xc * xc, axis=-1, keepdims=True)
        yh = xc * lax.rsqrt(var + EPS) * (0.5 * lng_ref[...]) + 0.5 * lnb_ref[...]
        v_scr[r:r + NORM_ROWS, :] = (yh * (1.0 + jnp.tanh(yh))).astype(jnp.bfloat16)

    for c0 in range(0, D, MXU_COLS):
        cols = slice(c0, c0 + MXU_COLS)
        ya = _dot(v_scr[...], cwo_ref[:, cols]) + cbo_ref[:, cols]
        m_scr[:, cols] = _sigmoid(in_proj(off_ga, c0)) * ya

    rx0 = SUBLANES * rx_halo
    for c0 in range(0, d_rnn, MXU_COLS):
        rx_scr[rx0:rx0 + T, c0:c0 + MXU_COLS] = in_proj(off_rx, c0)
    fill_halo(rx_scr, rxprev_scr, rx_halo)

    def emit_rx(r, lanes, acc):
        xb_scr[r:r + CONV_ROWS, lanes] = acc
        xbb_scr[r:r + CONV_ROWS, lanes] = acc.astype(jnp.bfloat16)
    causal_conv(rx_scr, lcw_ref, lcb_ref, emit_rx)

    lam = lam_ref[...]
    neg_c_sp = -LRU_C * (jnp.maximum(-lam, 0.0) + jnp.log1p(jnp.exp(-jnp.abs(lam))))

    sub = lax.broadcasted_iota(jnp.int32, (SUBLANES, MXU_COLS), 0)
    for j, ks in enumerate(gate_starts):
        c0 = j * MXU_COLS
        cols = slice(c0, c0 + MXU_COLS)
        g = _dot(xbb_scr[:, ks:ks + GATE_K], wg_ref[j])
        r_gate = _sigmoid(g[:, :MXU_COLS] + ba_ref[:, cols])
        i_gate = _sigmoid(g[:, MXU_COLS:] + bx_ref[:, cols])
        log_a = neg_c_sp[:, cols] * r_gate
        a = jnp.exp(log_a)
        m2 = -jnp.tanh(log_a) * (1.0 + a * a)
        mult = jnp.where(m2 > 0.0, m2 * lax.rsqrt(m2), 0.0)
        b = mult * (i_gate * xb_scr[:, cols])

        a3 = a.reshape(L, SUBLANES, MXU_COLS)
        b3 = b.reshape(L, SUBLANES, MXU_COLS)
        hloc, prod = [b3[0]], [a3[0]]
        for i in range(1, L):
            hloc.append(a3[i] * hloc[-1] + b3[i])
            prod.append(a3[i] * prod[-1])
        ea, eb = prod[-1], hloc[-1]
        d = 1
        while d < SUBLANES:
            keep = sub >= d
            ea_sh = jnp.where(keep, pltpu.roll(ea, d, 0), 1.0)
            eb_sh = jnp.where(keep, pltpu.roll(eb, d, 0), 0.0)
            eb = eb + ea * eb_sh
            ea = ea * ea_sh
            d *= 2
        h_in = hcar_scr[:, cols]
        ends = eb + ea * h_in
        start = jnp.where(sub == 0, h_in, pltpu.roll(ends, 1, 0))
        hcar_scr[:, cols] = jnp.broadcast_to(ends[SUBLANES - 1:SUBLANES, :], (SUBLANES, MXU_COLS))
        hseq = jnp.concatenate([hl + pr * start for hl, pr in zip(hloc, prod)], axis=0)

        y_scr[:, cols] = (hseq * _gelu_tanh_of_twice(in_proj(off_rg, c0))).astype(jnp.bfloat16)

    for c0 in range(0, D, MXU_COLS):
        cols = slice(c0, c0 + MXU_COLS)
        yb = _dot(y_scr[...], lwo_ref[:, cols])
        mb_scr[:, cols] = (m_scr[:, cols] + _sigmoid(in_proj(off_gb, c0)) * yb).astype(jnp.bfloat16)

    for c0 in range(0, D, MXU_COLS):
        res = _dot(mb_scr[...], wmix_ref[:, c0:c0 + MXU_COLS])
        for q in range(MXU_COLS // LANES):
            op_scr[c0 // LANES + q] = res[:, q * LANES:(q + 1) * LANES]
    for r in range(0, T, NORM_ROWS):
        for c in range(D // LANES):
            lanes = slice(c * LANES, (c + 1) * LANES)
            o_ref[r:r + NORM_ROWS, lanes] = (x_ref[r:r + NORM_ROWS, lanes]
                                             + op_scr[c, interleaved_rows(r, NORM_ROWS), :])


def _ffn_kernel(x_ref, n2g_ref, w1_ref, w3_ref, w2_ref, nfg_ref, o_ref, h_scr, act_scr, y_scr):
    tm, D = x_ref.shape
    d_ff = w1_ref.shape[1]

    for r in range(0, tm, NORM_ROWS):
        x = x_ref[r:r + NORM_ROWS, :]
        ms = jnp.mean(x * x, axis=-1, keepdims=True)
        h_scr[r:r + NORM_ROWS, :] = (x * lax.rsqrt(ms + EPS) * n2g_ref[...]).astype(jnp.bfloat16)

    for c0 in range(0, d_ff, MXU_COLS):
        cols = slice(c0, c0 + MXU_COLS)
        g = _dot(h_scr[...], w1_ref[:, cols])
        u = _dot(h_scr[...], w3_ref[:, cols])
        act_scr[:, cols] = (_silu(g) * u).astype(jnp.bfloat16)

    for c0 in range(0, D, MXU_COLS):
        cols = slice(c0, c0 + MXU_COLS)
        y_scr[:, cols] = x_ref[:, cols] + _dot(act_scr[...], w2_ref[:, cols])

    for r in range(0, tm, NORM_ROWS):
        y = y_scr[r:r + NORM_ROWS, :]
        ms = jnp.mean(y * y, axis=-1, keepdims=True)
        o_ref[r:r + NORM_ROWS, :] = y * lax.rsqrt(ms + EPS) * nfg_ref[...]


def _resident(shape):
    return pl.BlockSpec(shape, lambda *_: (0,) * len(shape), pipeline_mode=pl.Buffered(1))


def _gate_weights(wa, wx, gate_starts):
    heads, blk, _ = wa.shape
    d_rnn = heads * blk
    head_of = np.arange(d_rnn) // blk
    same_head = jnp.asarray(head_of[:, None] == head_of[None, :])

    def dense(w):
        return jnp.where(same_head, jnp.tile(w.reshape(d_rnn, blk), (1, heads)), 0.0)

    da, dx = dense(wa), dense(wx)
    tiles = []
    for j, ks in enumerate(gate_starts):
        cols = slice(j * MXU_COLS, (j + 1) * MXU_COLS)
        tiles.append(jnp.concatenate([da[ks:ks + GATE_K, cols], dx[ks:ks + GATE_K, cols]], axis=1))
    return jnp.stack(tiles).astype(jnp.bfloat16)


def kernel(x, norm1_g, w_in, b_in, conv_dw_w, conv_dw_b, conv_ln_g, conv_ln_b, conv_w_out, conv_b_out,
           lru_conv_w, lru_conv_b, lru_wa, lru_ba, lru_wx, lru_bx, lru_lambda, lru_w_out, w_mix_out,
           norm2_g, ffn_w1, ffn_w3, ffn_w2, norm_f_g):
    B, S, D = x.shape
    depth = w_in.shape[0]
    d_in = w_in.shape[2]
    conv_k, d_conv = conv_dw_w.shape[1:]
    lru_k, d_rnn = lru_conv_w.shape[1:]
    heads, blk = lru_wa.shape[1:3]
    d_ff = ffn_w1.shape[2]
    T = MIX_TILE
    assert S % T == 0 and (B * S) % FFN_TILE == 0
    assert d_in == 2 * d_conv + 2 * d_rnn + 2 * D and heads * blk == d_rnn
    assert d_conv % MXU_COLS == 0 and d_rnn % MXU_COLS == 0 and D % MXU_COLS == 0 and d_ff % MXU_COLS == 0
    u_halo_rows = SUBLANES * (conv_k - 1)
    rx_halo_rows = SUBLANES * (lru_k - 1)
    gate_starts = _gate_windows(d_rnn, blk)
    n_gate = len(gate_starts)
    bf16 = jnp.bfloat16
    f32 = jnp.float32

    def row(v):
        return v.reshape(1, -1)

    for l in range(depth):
        wg = _gate_weights(lru_wa[l], lru_wx[l], gate_starts)
        mixer = pl.pallas_call(
            functools.partial(_mixer_kernel, gate_starts=gate_starts),
            grid=(B, S // T),
            in_specs=[
                pl.BlockSpec((None, T, D), lambda b, s: (b, s, 0)),
                _resident((1, D)),
                _resident((D, d_in)), _resident((1, d_in)),
                _resident((conv_k, d_conv)), _resident((1, d_conv)),
                _resident((1, d_conv)), _resident((1, d_conv)),
                _resident((d_conv, D)), _resident((1, D)),
                _resident((lru_k, d_rnn)), _resident((1, d_rnn)),
                _resident((n_gate, GATE_K, 2 * MXU_COLS)),
                _resident((1, d_rnn)), _resident((1, d_rnn)), _resident((1, d_rnn)),
                _resident((d_rnn, D)), _resident((D, D)),
            ],
            out_specs=pl.BlockSpec((None, T, D), lambda b, s: (b, s, 0)),
            out_shape=jax.ShapeDtypeStruct((B, S, D), f32),
            scratch_shapes=[
                pltpu.VMEM((D // LANES, T, LANES), f32),
                pltpu.VMEM((T, D), bf16),
                pltpu.VMEM((u_halo_rows + T, d_conv), f32),
                pltpu.VMEM((u_halo_rows, d_conv), f32),
                pltpu.VMEM((T, d_conv), f32),
                pltpu.VMEM((T, d_conv), bf16),
                pltpu.VMEM((rx_halo_rows + T, d_rnn), f32),
                pltpu.VMEM((rx_halo_rows, d_rnn), f32),
                pltpu.VMEM((T, d_rnn), f32),
                pltpu.VMEM((T, d_rnn), bf16),
                pltpu.VMEM((T, d_rnn), bf16),
                pltpu.VMEM((T, D), f32),
                pltpu.VMEM((T, D), bf16),
                pltpu.VMEM((D // LANES, T, LANES), f32),
                pltpu.VMEM((SUBLANES, d_rnn), f32),
            ],
            compiler_params=pltpu.CompilerParams(
                dimension_semantics=("arbitrary", "arbitrary"),
                vmem_limit_bytes=MIX_VMEM_BYTES),
            name="token_mixer",
        )
        in_scale = jnp.asarray(np.concatenate([
            np.ones(2 * d_conv + d_rnn), np.full(d_rnn, 0.5), np.ones(2 * D)]).astype(np.float32))
        x = mixer(
            x, row(norm1_g[l]), (w_in[l] * in_scale).astype(bf16), row(b_in[l] * in_scale),
            conv_dw_w[l], row(conv_dw_b[l]), row(conv_ln_g[l]), row(conv_ln_b[l]),
            conv_w_out[l].astype(bf16), row(conv_b_out[l]),
            lru_conv_w[l], row(lru_conv_b[l]), wg, row(lru_ba[l]), row(lru_bx[l]), row(lru_lambda[l]),
            lru_w_out[l].astype(bf16), w_mix_out[l].astype(bf16))

        last = l == depth - 1
        ffn = pl.pallas_call(
            _ffn_kernel,
            grid=(B * S // FFN_TILE,),
            in_specs=[
                pl.BlockSpec((FFN_TILE, D), lambda i: (i, 0)),
                _resident((1, D)),
                _resident((D, d_ff)), _resident((D, d_ff)), _resident((d_ff, D)),
                _resident((1, D)),
            ],
            out_specs=pl.BlockSpec((FFN_TILE, D), lambda i: (i, 0)),
            out_shape=jax.ShapeDtypeStruct((B * S, D), f32),
            scratch_shapes=[
                pltpu.VMEM((FFN_TILE, D), bf16),
                pltpu.VMEM((FFN_TILE, d_ff), bf16),
                pltpu.VMEM((FFN_TILE, D), f32),
            ],
            compiler_params=pltpu.CompilerParams(
                dimension_semantics=("arbitrary",),
                vmem_limit_bytes=FFN_VMEM_BYTES),
            name="channel_mixer",
        )
        assert last, "the final rmsnorm is fused into the last layer's channel mixer"
        x = ffn(x.reshape(B * S, D), row(norm2_g[l]), ffn_w1[l].astype(bf16), ffn_w3[l].astype(bf16),
                ffn_w2[l].astype(bf16), row(norm_f_g)).reshape(B, S, D)
    return x
```

```python
import functools
import math

import jax
import jax.numpy as jnp
import numpy as np
from jax import lax
from jax.experimental import pallas as pl
from jax.experimental.pallas import tpu as pltpu

EPS = 1e-6
LRU_C = 8.0

SUBLANES = 8
LANES = 128
MXU_COLS = 256

MIX_TILE = 256
FFN_TILE = 512
CONV_ROWS = 64
CONV_LAG = 2
NORM_ROWS = 32
GATE_K = 512

MIX_VMEM_BYTES = 48 * 1024 * 1024
FFN_VMEM_BYTES = 44 * 1024 * 1024


def _sigmoid(x):
    return 0.5 * jnp.tanh(0.5 * x) + 0.5


def _silu(x):
    return x * _sigmoid(x)


def _twice_sigmoid_of_twice(zh):
    return 1.0 + jnp.tanh(zh)


def _gelu_tanh(x):
    c = math.sqrt(2.0 / math.pi)
    return 0.5 * x * (1.0 + jnp.tanh(c * (x + 0.044715 * (x * x * x))))


def _dot(a, b):
    return jnp.dot(a, b, preferred_element_type=jnp.float32)


def _gate_windows(d_rnn, block):
    starts = []
    for j in range(d_rnn // MXU_COLS):
        lo = (j * MXU_COLS // block) * block
        hi = -(-((j + 1) * MXU_COLS) // block) * block
        start = min(lo // LANES * LANES, d_rnn - GATE_K)
        assert start <= lo and hi <= start + GATE_K
        starts.append(start)
    return tuple(starts)


def _mixer_kernel(x_ref, n1g_ref, win_ref, bin_ref, cw_ref, cb_ref, lng_ref, lnb_ref, cwo_ref, cbo_ref,
                  lcw_ref, lcb_ref, wg_ref, ba_ref, bx_ref, lam_ref, lwo_ref, wmix_ref,
                  o_ref,
                  hp_scr, h_scr, u_scr, uprev_scr, c_scr, v_scr, rx_scr, rxprev_scr, xb_scr, xbb_scr,
                  y_scr, m_scr, mb_scr, op_scr, hcar_scr,
                  *, gate_starts):
    T, D = x_ref.shape
    L = T // SUBLANES
    d_conv = cwo_ref.shape[0]
    d_rnn = lwo_ref.shape[0]
    u_halo = cw_ref.shape[0] - 1
    rx_halo = lcw_ref.shape[0] - 1
    assert u_halo <= L and rx_halo <= L and L % NORM_ROWS == 0 and T % CONV_ROWS == 0
    off_glu_b = d_conv
    off_rx = 2 * d_conv
    off_rg = off_rx + d_rnn
    off_ga = off_rg + d_rnn
    off_gb = off_ga + D

    @pl.when(pl.program_id(1) == 0)
    def _():
        uprev_scr[...] = jnp.zeros(uprev_scr.shape, jnp.float32)
        rxprev_scr[...] = jnp.zeros(rxprev_scr.shape, jnp.float32)
        hcar_scr[...] = jnp.zeros(hcar_scr.shape, jnp.float32)

    def interleaved_rows(r, n):
        s, i0 = divmod(r, L)
        return pl.ds(SUBLANES * i0 + s, n, stride=SUBLANES)

    for r in range(0, T, NORM_ROWS):
        x = x_ref[r:r + NORM_ROWS, :]
        ms = jnp.mean(x * x, axis=-1, keepdims=True)
        h = x * lax.rsqrt(ms + EPS) * n1g_ref[...]
        for c in range(D // LANES):
            hp_scr[c, interleaved_rows(r, NORM_ROWS), :] = h[:, c * LANES:(c + 1) * LANES]
    for c in range(D // LANES):
        h_scr[:, c * LANES:(c + 1) * LANES] = hp_scr[c].astype(jnp.bfloat16)

    def in_proj(off, c0, width=MXU_COLS):
        cols = slice(off + c0, off + c0 + width)
        return _dot(h_scr[...], win_ref[:, cols]) + bin_ref[:, cols]

    def fill_halo(buf, prev, halo):
        n, width = SUBLANES * halo, buf.shape[1]
        tail = buf[T:T + n, :].reshape(halo, SUBLANES, width)
        sub = lax.broadcasted_iota(jnp.int32, tail.shape, 1)
        mixed = jnp.where(sub == SUBLANES - 1, prev[...].reshape(tail.shape), tail)
        buf[0:n, :] = pltpu.roll(mixed, 1, 1).reshape(n, width)
        prev[...] = tail.reshape(n, width)

    always = pl.program_id(1) >= 0

    def causal_conv(buf, w_ref, b_ref, emit):
        groups = CONV_ROWS // SUBLANES
        prev = None
        for l0 in range(0, buf.shape[1], LANES):
            lanes = slice(l0, l0 + LANES)
            for r in range(0, T, CONV_ROWS):
                acc = jnp.broadcast_to(b_ref[:, lanes], (groups, SUBLANES, LANES))
                if prev is not None:
                    acc = jnp.where(always, acc, prev)
                sums = [acc[0]]
                for k in range(w_ref.shape[0]):
                    row = r + SUBLANES * k
                    wk = jnp.broadcast_to(w_ref[k:k + 1, lanes], (SUBLANES, LANES))
                    if k >= CONV_LAG:
                        wk = jnp.where(always, wk, sums[k - CONV_LAG])
                    acc = acc + wk[None] * buf[row:row + CONV_ROWS, lanes].reshape(groups, SUBLANES, LANES)
                    sums.append(acc[0])
                emit(r, lanes, acc.reshape(CONV_ROWS, LANES))
                prev = acc

    u0 = SUBLANES * u_halo
    for c0 in range(0, d_conv, MXU_COLS):
        za = in_proj(0, c0)
        zb = in_proj(off_glu_b, c0)
        u_scr[u0:u0 + T, c0:c0 + MXU_COLS] = za * _twice_sigmoid_of_twice(zb)
    fill_halo(u_scr, uprev_scr, u_halo)

    def emit_conv(r, lanes, acc):
        c_scr[r:r + CONV_ROWS, lanes] = acc
    causal_conv(u_scr, cw_ref, cb_ref, emit_conv)

    for r in range(0, T, NORM_ROWS):
        cv = c_scr[r:r + NORM_ROWS, :]
        mu = jnp.mean(cv, axis=-1, keepdims=True)
        xc = cv - mu
        var = jnp.mean(xc * xc, axis=-1, keepdims=True)
        y = xc * lax.rsqrt(var + EPS) * lng_ref[...] + lnb_ref[...]
        v_scr[r:r + NORM_ROWS, :] = _silu(y).astype(jnp.bfloat16)

    for c0 in range(0, D, MXU_COLS):
        cols = slice(c0, c0 + MXU_COLS)
        ya = _dot(v_scr[...], cwo_ref[:, cols]) + 0.5 * cbo_ref[:, cols]
        m_scr[:, cols] = _twice_sigmoid_of_twice(in_proj(off_ga, c0)) * ya

    rx0 = SUBLANES * rx_halo
    for c0 in range(0, d_rnn, MXU_COLS):
        rx_scr[rx0:rx0 + T, c0:c0 + MXU_COLS] = in_proj(off_rx, c0)
    fill_halo(rx_scr, rxprev_scr, rx_halo)

    def emit_rx(r, lanes, acc):
        xb_scr[r:r + CONV_ROWS, lanes] = acc
        xbb_scr[r:r + CONV_ROWS, lanes] = acc.astype(jnp.bfloat16)
    causal_conv(rx_scr, lcw_ref, lcb_ref, emit_rx)

    lam = lam_ref[...]
    half_neg_c_sp = (-0.5 * LRU_C) * (jnp.maximum(-lam, 0.0) + jnp.log1p(jnp.exp(-jnp.abs(lam))))

    sub = lax.broadcasted_iota(jnp.int32, (SUBLANES, MXU_COLS), 0)
    for j, ks in enumerate(gate_starts):
        c0 = j * MXU_COLS
        cols = slice(c0, c0 + MXU_COLS)
        g = _dot(xbb_scr[:, ks:ks + GATE_K], wg_ref[j])
        r_gate2 = _twice_sigmoid_of_twice(g[:, :MXU_COLS] + 0.5 * ba_ref[:, cols])
        i_gate = 0.5 * jnp.tanh(g[:, MXU_COLS:] + 0.5 * bx_ref[:, cols]) + 0.5
        log_a = half_neg_c_sp[:, cols] * r_gate2
        a = jnp.exp(log_a)
        mult = jnp.sqrt(-jnp.tanh(log_a) * (1.0 + a * a))
        b = mult * (i_gate * xb_scr[:, cols])

        a3 = a.reshape(L, SUBLANES, MXU_COLS)
        b3 = b.reshape(L, SUBLANES, MXU_COLS)
        hloc, prod = [b3[0]], [a3[0]]
        for i in range(1, L):
            hloc.append(a3[i] * hloc[-1] + b3[i])
            prod.append(a3[i] * prod[-1])
        ea, eb = prod[-1], hloc[-1]
        d = 1
        while d < SUBLANES:
            keep = sub >= d
            ea_sh = jnp.where(keep, pltpu.roll(ea, d, 0), 1.0)
            eb_sh = jnp.where(keep, pltpu.roll(eb, d, 0), 0.0)
            eb = eb + ea * eb_sh
            ea = ea * ea_sh
            d *= 2
        h_in = hcar_scr[:, cols]
        ends = eb + ea * h_in
        start = jnp.where(sub == 0, h_in, pltpu.roll(ends, 1, 0))
        hcar_scr[:, cols] = jnp.broadcast_to(ends[SUBLANES - 1:SUBLANES, :], (SUBLANES, MXU_COLS))
        hseq = jnp.concatenate([hl + pr * start for hl, pr in zip(hloc, prod)], axis=0)

        y_scr[:, cols] = (hseq * _gelu_tanh(in_proj(off_rg, c0))).astype(jnp.bfloat16)

    for c0 in range(0, D, MXU_COLS):
        cols = slice(c0, c0 + MXU_COLS)
        yb = _dot(y_scr[...], lwo_ref[:, cols])
        mb_scr[:, cols] = (m_scr[:, cols] + _twice_sigmoid_of_twice(in_proj(off_gb, c0)) * yb
                           ).astype(jnp.bfloat16)

    for c0 in range(0, D, MXU_COLS):
        res = _dot(mb_scr[...], wmix_ref[:, c0:c0 + MXU_COLS])
        for q in range(MXU_COLS // LANES):
            op_scr[c0 // LANES + q] = res[:, q * LANES:(q + 1) * LANES]
    for r in range(0, T, NORM_ROWS):
        for c in range(D // LANES):
            lanes = slice(c * LANES, (c + 1) * LANES)
            o_ref[r:r + NORM_ROWS, lanes] = (x_ref[r:r + NORM_ROWS, lanes]
                                             + op_scr[c, interleaved_rows(r, NORM_ROWS), :])


def _ffn_kernel(x_ref, n2g_ref, w1_ref, w3_ref, w2_ref, nfg_ref, o_ref, h_scr, act_scr, y_scr):
    tm, D = x_ref.shape
    d_ff = w1_ref.shape[1]

    for r in range(0, tm, NORM_ROWS):
        x = x_ref[r:r + NORM_ROWS, :]
        ms = jnp.mean(x * x, axis=-1, keepdims=True)
        h_scr[r:r + NORM_ROWS, :] = (x * lax.rsqrt(ms + EPS) * n2g_ref[...]).astype(jnp.bfloat16)

    for c0 in range(0, d_ff, MXU_COLS):
        cols = slice(c0, c0 + MXU_COLS)
        g = _dot(h_scr[...], w1_ref[:, cols])
        u = _dot(h_scr[...], w3_ref[:, cols])
        act_scr[:, cols] = (_silu(g) * u).astype(jnp.bfloat16)

    for c0 in range(0, D, MXU_COLS):
        cols = slice(c0, c0 + MXU_COLS)
        y_scr[:, cols] = x_ref[:, cols] + _dot(act_scr[...], w2_ref[:, cols])

    for r in range(0, tm, NORM_ROWS):
        y = y_scr[r:r + NORM_ROWS, :]
        ms = jnp.mean(y * y, axis=-1, keepdims=True)
        o_ref[r:r + NORM_ROWS, :] = y * lax.rsqrt(ms + EPS) * nfg_ref[...]


def _resident(shape):
    return pl.BlockSpec(shape, lambda *_: (0,) * len(shape), pipeline_mode=pl.Buffered(1))


def _gate_weights(wa, wx, gate_starts):
    heads, blk, _ = wa.shape
    d_rnn = heads * blk
    head_of = np.arange(d_rnn) // blk
    same_head = jnp.asarray(head_of[:, None] == head_of[None, :])

    def dense(w):
        return jnp.where(same_head, jnp.tile(w.reshape(d_rnn, blk), (1, heads)), 0.0)

    da, dx = dense(wa), dense(wx)
    tiles = []
    for j, ks in enumerate(gate_starts):
        cols = slice(j * MXU_COLS, (j + 1) * MXU_COLS)
        tiles.append(jnp.concatenate([da[ks:ks + GATE_K, cols], dx[ks:ks + GATE_K, cols]], axis=1))
    return (0.5 * jnp.stack(tiles)).astype(jnp.bfloat16)


def kernel(x, norm1_g, w_in, b_in, conv_dw_w, conv_dw_b, conv_ln_g, conv_ln_b, conv_w_out, conv_b_out,
           lru_conv_w, lru_conv_b, lru_wa, lru_ba, lru_wx, lru_bx, lru_lambda, lru_w_out, w_mix_out,
           norm2_g, ffn_w1, ffn_w3, ffn_w2, norm_f_g):
    B, S, D = x.shape
    depth = w_in.shape[0]
    d_in = w_in.shape[2]
    conv_k, d_conv = conv_dw_w.shape[1:]
    lru_k, d_rnn = lru_conv_w.shape[1:]
    heads, blk = lru_wa.shape[1:3]
    d_ff = ffn_w1.shape[2]
    T = MIX_TILE
    assert S % T == 0 and (B * S) % FFN_TILE == 0
    assert d_in == 2 * d_conv + 2 * d_rnn + 2 * D and heads * blk == d_rnn
    assert d_conv % MXU_COLS == 0 and d_rnn % MXU_COLS == 0 and D % MXU_COLS == 0 and d_ff % MXU_COLS == 0
    u_halo_rows = SUBLANES * (conv_k - 1)
    rx_halo_rows = SUBLANES * (lru_k - 1)
    gate_starts = _gate_windows(d_rnn, blk)
    n_gate = len(gate_starts)
    bf16 = jnp.bfloat16
    f32 = jnp.float32

    def row(v):
        return v.reshape(1, -1)

    for l in range(depth):
        wg = _gate_weights(lru_wa[l], lru_wx[l], gate_starts)
        mixer = pl.pallas_call(
            functools.partial(_mixer_kernel, gate_starts=gate_starts),
            grid=(B, S // T),
            in_specs=[
                pl.BlockSpec((None, T, D), lambda b, s: (b, s, 0)),
                _resident((1, D)),
                _resident((D, d_in)), _resident((1, d_in)),
                _resident((conv_k, d_conv)), _resident((1, d_conv)),
                _resident((1, d_conv)), _resident((1, d_conv)),
                _resident((d_conv, D)), _resident((1, D)),
                _resident((lru_k, d_rnn)), _resident((1, d_rnn)),
                _resident((n_gate, GATE_K, 2 * MXU_COLS)),
                _resident((1, d_rnn)), _resident((1, d_rnn)), _resident((1, d_rnn)),
                _resident((d_rnn, D)), _resident((D, D)),
            ],
            out_specs=pl.BlockSpec((None, T, D), lambda b, s: (b, s, 0)),
            out_shape=jax.ShapeDtypeStruct((B, S, D), f32),
            scratch_shapes=[
                pltpu.VMEM((D // LANES, T, LANES), f32),
                pltpu.VMEM((T, D), bf16),
                pltpu.VMEM((u_halo_rows + T, d_conv), f32),
                pltpu.VMEM((u_halo_rows, d_conv), f32),
                pltpu.VMEM((T, d_conv), f32),
                pltpu.VMEM((T, d_conv), bf16),
                pltpu.VMEM((rx_halo_rows + T, d_rnn), f32),
                pltpu.VMEM((rx_halo_rows, d_rnn), f32),
                pltpu.VMEM((T, d_rnn), f32),
                pltpu.VMEM((T, d_rnn), bf16),
                pltpu.VMEM((T, d_rnn), bf16),
                pltpu.VMEM((T, D), f32),
                pltpu.VMEM((T, D), bf16),
                pltpu.VMEM((D // LANES, T, LANES), f32),
                pltpu.VMEM((SUBLANES, d_rnn), f32),
            ],
            compiler_params=pltpu.CompilerParams(
                dimension_semantics=("arbitrary", "arbitrary"),
                vmem_limit_bytes=MIX_VMEM_BYTES),
            name="token_mixer",
        )
        in_scale = jnp.asarray(np.concatenate([
            np.full(2 * d_conv, 0.5), np.ones(2 * d_rnn), np.full(2 * D, 0.5)]).astype(np.float32))
        x = mixer(
            x, row(norm1_g[l]), (w_in[l] * in_scale).astype(bf16), row(b_in[l] * in_scale),
            conv_dw_w[l], row(conv_dw_b[l]), row(conv_ln_g[l]), row(conv_ln_b[l]),
            (0.5 * conv_w_out[l]).astype(bf16), row(conv_b_out[l]),
            lru_conv_w[l], row(lru_conv_b[l]), wg, row(lru_ba[l]), row(lru_bx[l]), row(lru_lambda[l]),
            (0.5 * lru_w_out[l]).astype(bf16), w_mix_out[l].astype(bf16))

        last = l == depth - 1
        ffn = pl.pallas_call(
            _ffn_kernel,
            grid=(B * S // FFN_TILE,),
            in_specs=[
                pl.BlockSpec((FFN_TILE, D), lambda i: (i, 0)),
                _resident((1, D)),
                _resident((D, d_ff)), _resident((D, d_ff)), _resident((d_ff, D)),
                _resident((1, D)),
            ],
            out_specs=pl.BlockSpec((FFN_TILE, D), lambda i: (i, 0)),
            out_shape=jax.ShapeDtypeStruct((B * S, D), f32),
            scratch_shapes=[
                pltpu.VMEM((FFN_TILE, D), bf16),
                pltpu.VMEM((FFN_TILE, d_ff), bf16),
                pltpu.VMEM((FFN_TILE, D), f32),
            ],
            compiler_params=pltpu.CompilerParams(
                dimension_semantics=("arbitrary",),
                vmem_limit_bytes=FFN_VMEM_BYTES),
            name="channel_mixer",
        )
        assert last, "the final rmsnorm is fused into the last layer's channel mixer"
        x = ffn(x.reshape(B * S, D), row(norm2_g[l]), ffn_w1[l].astype(bf16), ffn_w3[l].astype(bf16),
                ffn_w2[l].astype(bf16), row(norm_f_g)).reshape(B, S, D)
    return x
```

```python
import functools
import math

import jax
import jax.numpy as jnp
import numpy as np
from jax import lax
from jax.experimental import pallas as pl
from jax.experimental.pallas import tpu as pltpu

EPS = 1e-6
LRU_C = 8.0

SUBLANES = 8
LANES = 128
MXU_COLS = 256

MIX_TILE = 256
FFN_TILE = 1024
CONV_ROWS = 64
CONV_LAG = 2
NORM_ROWS = 32
GATE_K = 512

MIX_VMEM_BYTES = 48 * 1024 * 1024
FFN_VMEM_BYTES = 56 * 1024 * 1024


def _sigmoid(x):
    return 0.5 * jnp.tanh(0.5 * x) + 0.5


def _silu(x):
    return x * _sigmoid(x)


def _gelu_tanh(x):
    c = math.sqrt(2.0 / math.pi)
    return 0.5 * x * (1.0 + jnp.tanh(c * (x + 0.044715 * (x * x * x))))


def _dot(a, b):
    return jnp.dot(a, b, preferred_element_type=jnp.float32)


def _gate_windows(d_rnn, block):
    starts = []
    for j in range(d_rnn // MXU_COLS):
        lo = (j * MXU_COLS // block) * block
        hi = -(-((j + 1) * MXU_COLS) // block) * block
        start = min(lo // LANES * LANES, d_rnn - GATE_K)
        assert start <= lo and hi <= start + GATE_K
        starts.append(start)
    return tuple(starts)


def _mixer_kernel(x_ref, n1g_ref, win_ref, bin_ref, cw_ref, cb_ref, lng_ref, lnb_ref, cwo_ref, cbo_ref,
                  lcw_ref, lcb_ref, wg_ref, ba_ref, bx_ref, lam_ref, lwo_ref, wmix_ref,
                  o_ref,
                  hp_scr, h_scr, u_scr, uprev_scr, c_scr, v_scr, rx_scr, rxprev_scr, xb_scr, xbb_scr,
                  y_scr, m_scr, mb_scr, op_scr, hcar_scr,
                  *, gate_starts):
    T, D = x_ref.shape
    L = T // SUBLANES
    d_conv = cwo_ref.shape[0]
    d_rnn = lwo_ref.shape[0]
    u_halo = cw_ref.shape[0] - 1
    rx_halo = lcw_ref.shape[0] - 1
    assert u_halo <= L and rx_halo <= L and L % NORM_ROWS == 0 and T % CONV_ROWS == 0
    off_glu_b = d_conv
    off_rx = 2 * d_conv
    off_rg = off_rx + d_rnn
    off_ga = off_rg + d_rnn
    off_gb = off_ga + D

    @pl.when(pl.program_id(1) == 0)
    def _():
        uprev_scr[...] = jnp.zeros(uprev_scr.shape, jnp.float32)
        rxprev_scr[...] = jnp.zeros(rxprev_scr.shape, jnp.float32)
        hcar_scr[...] = jnp.zeros(hcar_scr.shape, jnp.float32)

    def interleaved_rows(r, n):
        s, i0 = divmod(r, L)
        return pl.ds(SUBLANES * i0 + s, n, stride=SUBLANES)

    for r in range(0, T, NORM_ROWS):
        x = x_ref[r:r + NORM_ROWS, :]
        ms = jnp.mean(x * x, axis=-1, keepdims=True)
        h = x * lax.rsqrt(ms + EPS) * n1g_ref[...]
        for c in range(D // LANES):
            hp_scr[c, interleaved_rows(r, NORM_ROWS), :] = h[:, c * LANES:(c + 1) * LANES]
    for c in range(D // LANES):
        h_scr[:, c * LANES:(c + 1) * LANES] = hp_scr[c].astype(jnp.bfloat16)

    def in_proj(off, c0, width=MXU_COLS):
        cols = slice(off + c0, off + c0 + width)
        return _dot(h_scr[...], win_ref[:, cols]) + bin_ref[:, cols]

    def fill_halo(buf, prev, halo):
        n, width = SUBLANES * halo, buf.shape[1]
        tail = buf[T:T + n, :].reshape(halo, SUBLANES, width)
        sub = lax.broadcasted_iota(jnp.int32, tail.shape, 1)
        mixed = jnp.where(sub == SUBLANES - 1, prev[...].reshape(tail.shape), tail)
        buf[0:n, :] = pltpu.roll(mixed, 1, 1).reshape(n, width)
        prev[...] = tail.reshape(n, width)

    always = pl.program_id(1) >= 0

    def causal_conv(buf, w_ref, b_ref, emit):
        groups = CONV_ROWS // SUBLANES
        prev = None
        for l0 in range(0, buf.shape[1], LANES):
            lanes = slice(l0, l0 + LANES)
            for r in range(0, T, CONV_ROWS):
                acc = jnp.broadcast_to(b_ref[:, lanes], (groups, SUBLANES, LANES))
                if prev is not None:
                    acc = jnp.where(always, acc, prev)
                sums = [acc[0]]
                for k in range(w_ref.shape[0]):
                    row = r + SUBLANES * k
                    wk = jnp.broadcast_to(w_ref[k:k + 1, lanes], (SUBLANES, LANES))
                    if k >= CONV_LAG:
                        wk = jnp.where(always, wk, sums[k - CONV_LAG])
                    acc = acc + wk[None] * buf[row:row + CONV_ROWS, lanes].reshape(groups, SUBLANES, LANES)
                    sums.append(acc[0])
                emit(r, lanes, acc.reshape(CONV_ROWS, LANES))
                prev = acc

    u0 = SUBLANES * u_halo
    for c0 in range(0, d_conv, MXU_COLS):
        za = in_proj(0, c0)
        zb = in_proj(off_glu_b, c0)
        u_scr[u0:u0 + T, c0:c0 + MXU_COLS] = za * _sigmoid(zb)
    fill_halo(u_scr, uprev_scr, u_halo)

    def emit_conv(r, lanes, acc):
        c_scr[r:r + CONV_ROWS, lanes] = acc
    causal_conv(u_scr, cw_ref, cb_ref, emit_conv)

    for r in range(0, T, NORM_ROWS):
        cv = c_scr[r:r + NORM_ROWS, :]
        mu = jnp.mean(cv, axis=-1, keepdims=True)
        xc = cv - mu
        var = jnp.mean(xc * xc, axis=-1, keepdims=True)
        y = xc * lax.rsqrt(var + EPS) * lng_ref[...] + lnb_ref[...]
        v_scr[r:r + NORM_ROWS, :] = _silu(y).astype(jnp.bfloat16)

    for c0 in range(0, D, MXU_COLS):
        cols = slice(c0, c0 + MXU_COLS)
        ya = _dot(v_scr[...], cwo_ref[:, cols]) + cbo_ref[:, cols]
        m_scr[:, cols] = _sigmoid(in_proj(off_ga, c0)) * ya

    rx0 = SUBLANES * rx_halo
    for c0 in range(0, d_rnn, MXU_COLS):
        rx_scr[rx0:rx0 + T, c0:c0 + MXU_COLS] = in_proj(off_rx, c0)
    fill_halo(rx_scr, rxprev_scr, rx_halo)

    def emit_rx(r, lanes, acc):
        xb_scr[r:r + CONV_ROWS, lanes] = acc
        xbb_scr[r:r + CONV_ROWS, lanes] = acc.astype(jnp.bfloat16)
    causal_conv(rx_scr, lcw_ref, lcb_ref, emit_rx)

    lam = lam_ref[...]
    neg_c_sp = -LRU_C * (jnp.maximum(-lam, 0.0) + jnp.log1p(jnp.exp(-jnp.abs(lam))))

    sub = lax.broadcasted_iota(jnp.int32, (SUBLANES, MXU_COLS), 0)
    for j, ks in enumerate(gate_starts):
        c0 = j * MXU_COLS
        cols = slice(c0, c0 + MXU_COLS)
        g = _dot(xbb_scr[:, ks:ks + GATE_K], wg_ref[j])
        r_gate = _sigmoid(g[:, :MXU_COLS] + ba_ref[:, cols])
        i_gate = _sigmoid(g[:, MXU_COLS:] + bx_ref[:, cols])
        log_a = neg_c_sp[:, cols] * r_gate
        a = jnp.exp(log_a)
        mult = jnp.sqrt(-jnp.tanh(log_a) * (1.0 + a * a))
        b = mult * (i_gate * xb_scr[:, cols])

        a3 = a.reshape(L, SUBLANES, MXU_COLS)
        b3 = b.reshape(L, SUBLANES, MXU_COLS)
        hloc, prod = [b3[0]], [a3[0]]
        for i in range(1, L):
            hloc.append(a3[i] * hloc[-1] + b3[i])
            prod.append(a3[i] * prod[-1])
        ea, eb = prod[-1], hloc[-1]
        d = 1
        while d < SUBLANES:
            keep = sub >= d
            ea_sh = jnp.where(keep, pltpu.roll(ea, d, 0), 1.0)
            eb_sh = jnp.where(keep, pltpu.roll(eb, d, 0), 0.0)
            eb = eb + ea * eb_sh
            ea = ea * ea_sh
            d *= 2
        h_in = hcar_scr[:, cols]
        ends = eb + ea * h_in
        start = jnp.where(sub == 0, h_in, pltpu.roll(ends, 1, 0))
        hcar_scr[:, cols] = jnp.broadcast_to(ends[SUBLANES - 1:SUBLANES, :], (SUBLANES, MXU_COLS))
        hseq = jnp.concatenate([hl + pr * start for hl, pr in zip(hloc, prod)], axis=0)

        y_scr[:, cols] = (hseq * _gelu_tanh(in_proj(off_rg, c0))).astype(jnp.bfloat16)

    for c0 in range(0, D, MXU_COLS):
        cols = slice(c0, c0 + MXU_COLS)
        yb = _dot(y_scr[...], lwo_ref[:, cols])
        mb_scr[:, cols] = (m_scr[:, cols] + _sigmoid(in_proj(off_gb, c0)) * yb).astype(jnp.bfloat16)

    for c0 in range(0, D, MXU_COLS):
        res = _dot(mb_scr[...], wmix_ref[:, c0:c0 + MXU_COLS])
        for q in range(MXU_COLS // LANES):
            op_scr[c0 // LANES + q] = res[:, q * LANES:(q + 1) * LANES]
    for r in range(0, T, NORM_ROWS):
        for c in range(D // LANES):
            lanes = slice(c * LANES, (c + 1) * LANES)
            o_ref[r:r + NORM_ROWS, lanes] = (x_ref[r:r + NORM_ROWS, lanes]
                                             + op_scr[c, interleaved_rows(r, NORM_ROWS), :])


def _ffn_kernel(x_ref, n2g_ref, w1_ref, w3_ref, w2_ref, nfg_ref, o_ref, h_scr, act_scr, y_scr):
    tm, D = x_ref.shape
    d_ff = w1_ref.shape[1]

    for r in range(0, tm, NORM_ROWS):
        x = x_ref[r:r + NORM_ROWS, :]
        ms = jnp.mean(x * x, axis=-1, keepdims=True)
        h_scr[r:r + NORM_ROWS, :] = (x * lax.rsqrt(ms + EPS) * n2g_ref[...]).astype(jnp.bfloat16)

    for c0 in range(0, d_ff, MXU_COLS):
        cols = slice(c0, c0 + MXU_COLS)
        g = _dot(h_scr[...], w1_ref[:, cols])
        u = _dot(h_scr[...], w3_ref[:, cols])
        act_scr[:, cols] = (_silu(g) * u).astype(jnp.bfloat16)

    for c0 in range(0, D, MXU_COLS):
        cols = slice(c0, c0 + MXU_COLS)
        y_scr[:, cols] = x_ref[:, cols] + _dot(act_scr[...], w2_ref[:, cols])

    for r in range(0, tm, NORM_ROWS):
        y = y_scr[r:r + NORM_ROWS, :]
        ms = jnp.mean(y * y, axis=-1, keepdims=True)
        o_ref[r:r + NORM_ROWS, :] = y * lax.rsqrt(ms + EPS) * nfg_ref[...]


def _resident(shape):
    return pl.BlockSpec(shape, lambda *_: (0,) * len(shape), pipeline_mode=pl.Buffered(1))


def _layer_resident(shape, layer):
    return pl.BlockSpec((None,) + tuple(shape), lambda *_: (layer,) + (0,) * len(shape),
                        pipeline_mode=pl.Buffered(1))


def _stacked_rows(p):
    return p.reshape(p.shape[0], 1, p.shape[1])


def _gate_weights(wa, wx, gate_starts):
    heads, blk, _ = wa.shape
    d_rnn = heads * blk
    head_of = np.arange(d_rnn) // blk
    same_head = jnp.asarray(head_of[:, None] == head_of[None, :])

    def dense(w):
        return jnp.where(same_head, jnp.tile(w.reshape(d_rnn, blk), (1, heads)), 0.0)

    da, dx = dense(wa), dense(wx)
    tiles = []
    for j, ks in enumerate(gate_starts):
        cols = slice(j * MXU_COLS, (j + 1) * MXU_COLS)
        tiles.append(jnp.concatenate([da[ks:ks + GATE_K, cols], dx[ks:ks + GATE_K, cols]], axis=1))
    return jnp.stack(tiles).astype(jnp.bfloat16)


def kernel(x, norm1_g, w_in, b_in, conv_dw_w, conv_dw_b, conv_ln_g, conv_ln_b, conv_w_out, conv_b_out,
           lru_conv_w, lru_conv_b, lru_wa, lru_ba, lru_wx, lru_bx, lru_lambda, lru_w_out, w_mix_out,
           norm2_g, ffn_w1, ffn_w3, ffn_w2, norm_f_g):
    B, S, D = x.shape
    depth = w_in.shape[0]
    d_in = w_in.shape[2]
    conv_k, d_conv = conv_dw_w.shape[1:]
    lru_k, d_rnn = lru_conv_w.shape[1:]
    heads, blk = lru_wa.shape[1:3]
    d_ff = ffn_w1.shape[2]
    T = MIX_TILE
    assert S % T == 0 and (B * S) % FFN_TILE == 0
    assert d_in == 2 * d_conv + 2 * d_rnn + 2 * D and heads * blk == d_rnn
    assert d_conv % MXU_COLS == 0 and d_rnn % MXU_COLS == 0 and D % MXU_COLS == 0 and d_ff % MXU_COLS == 0
    u_halo_rows = SUBLANES * (conv_k - 1)
    rx_halo_rows = SUBLANES * (lru_k - 1)
    gate_starts = _gate_windows(d_rnn, blk)
    n_gate = len(gate_starts)
    bf16 = jnp.bfloat16
    f32 = jnp.float32

    def row(v):
        return v.reshape(1, -1)

    for l in range(depth):
        wg = _gate_weights(lru_wa[l], lru_wx[l], gate_starts)
        mixer = pl.pallas_call(
            functools.partial(_mixer_kernel, gate_starts=gate_starts),
            grid=(B, S // T),
            in_specs=[
                pl.BlockSpec((None, T, D), lambda b, s: (b, s, 0)),
                _layer_resident((1, D), l),
                _resident((D, d_in)), _layer_resident((1, d_in), l),
                _layer_resident((conv_k, d_conv), l), _layer_resident((1, d_conv), l),
                _layer_resident((1, d_conv), l), _layer_resident((1, d_conv), l),
                _resident((d_conv, D)), _layer_resident((1, D), l),
                _layer_resident((lru_k, d_rnn), l), _layer_resident((1, d_rnn), l),
                _resident((n_gate, GATE_K, 2 * MXU_COLS)),
                _layer_resident((1, d_rnn), l), _layer_resident((1, d_rnn), l), _layer_resident((1, d_rnn), l),
                _resident((d_rnn, D)), _resident((D, D)),
            ],
            out_specs=pl.BlockSpec((None, T, D), lambda b, s: (b, s, 0)),
            out_shape=jax.ShapeDtypeStruct((B, S, D), f32),
            scratch_shapes=[
                pltpu.VMEM((D // LANES, T, LANES), f32),
                pltpu.VMEM((T, D), bf16),
                pltpu.VMEM((u_halo_rows + T, d_conv), f32),
                pltpu.VMEM((u_halo_rows, d_conv), f32),
                pltpu.VMEM((T, d_conv), f32),
                pltpu.VMEM((T, d_conv), bf16),
                pltpu.VMEM((rx_halo_rows + T, d_rnn), f32),
                pltpu.VMEM((rx_halo_rows, d_rnn), f32),
                pltpu.VMEM((T, d_rnn), f32),
                pltpu.VMEM((T, d_rnn), bf16),
                pltpu.VMEM((T, d_rnn), bf16),
                pltpu.VMEM((T, D), f32),
                pltpu.VMEM((T, D), bf16),
                pltpu.VMEM((D // LANES, T, LANES), f32),
                pltpu.VMEM((SUBLANES, d_rnn), f32),
            ],
            compiler_params=pltpu.CompilerParams(
                dimension_semantics=("arbitrary", "arbitrary"),
                vmem_limit_bytes=MIX_VMEM_BYTES),
            name="token_mixer",
        )
        x = mixer(
            x, _stacked_rows(norm1_g), w_in[l].astype(bf16), _stacked_rows(b_in),
            conv_dw_w, _stacked_rows(conv_dw_b), _stacked_rows(conv_ln_g), _stacked_rows(conv_ln_b),
            conv_w_out[l].astype(bf16), _stacked_rows(conv_b_out),
            lru_conv_w, _stacked_rows(lru_conv_b), wg, _stacked_rows(lru_ba), _stacked_rows(lru_bx),
            _stacked_rows(lru_lambda),
            lru_w_out[l].astype(bf16), w_mix_out[l].astype(bf16))

        last = l == depth - 1
        ffn = pl.pallas_call(
            _ffn_kernel,
            grid=(B * S // FFN_TILE,),
            in_specs=[
                pl.BlockSpec((FFN_TILE, D), lambda i: (i, 0)),
                _layer_resident((1, D), l),
                _resident((D, d_ff)), _resident((D, d_ff)), _resident((d_ff, D)),
                _resident((1, D)),
            ],
            out_specs=pl.BlockSpec((FFN_TILE, D), lambda i: (i, 0)),
            out_shape=jax.ShapeDtypeStruct((B * S, D), f32),
            scratch_shapes=[
                pltpu.VMEM((FFN_TILE, D), bf16),
                pltpu.VMEM((FFN_TILE, d_ff), bf16),
                pltpu.VMEM((FFN_TILE, D), f32),
            ],
            compiler_params=pltpu.CompilerParams(
                dimension_semantics=("arbitrary",),
                vmem_limit_bytes=FFN_VMEM_BYTES),
            name="channel_mixer",
        )
        assert last, "the final rmsnorm is fused into the last layer's channel mixer"
        x = ffn(x.reshape(B * S, D), _stacked_rows(norm2_g), ffn_w1[l].astype(bf16), ffn_w3[l].astype(bf16),
                ffn_w2[l].astype(bf16), row(norm_f_g)).reshape(B, S, D)
    return x
```

```python
import functools
import math

import jax
import jax.numpy as jnp
import numpy as np
from jax import lax
from jax.experimental import pallas as pl
from jax.experimental.pallas import tpu as pltpu

EPS = 1e-6
LRU_C = 8.0

SUBLANES = 8
LANES = 128
MXU_COLS = 256

MIX_TILE = 512
FFN_TILE = 1024
CONV_ROWS = 64
CONV_LAG = 2
NORM_ROWS = 32
GATE_K = 512

MIX_VMEM_BYTES = 60 * 1024 * 1024
FFN_VMEM_BYTES = 56 * 1024 * 1024


def _sigmoid(x):
    return 0.5 * jnp.tanh(0.5 * x) + 0.5


def _silu(x):
    return x * _sigmoid(x)


def _gelu_tanh(x):
    c = math.sqrt(2.0 / math.pi)
    return 0.5 * x * (1.0 + jnp.tanh(c * (x + 0.044715 * (x * x * x))))


def _dot(a, b):
    return jnp.dot(a, b, preferred_element_type=jnp.float32)


def _gate_windows(d_rnn, block):
    starts = []
    for j in range(d_rnn // MXU_COLS):
        lo = (j * MXU_COLS // block) * block
        hi = -(-((j + 1) * MXU_COLS) // block) * block
        start = min(lo // LANES * LANES, d_rnn - GATE_K)
        assert start <= lo and hi <= start + GATE_K
        starts.append(start)
    return tuple(starts)


def _mixer_kernel(x_ref, n1g_ref, win_ref, bin_ref, cw_ref, cb_ref, lng_ref, lnb_ref, cwo_ref, cbo_ref,
                  lcw_ref, lcb_ref, wg_ref, ba_ref, bx_ref, lam_ref, lwo_ref, wmix_ref,
                  o_ref,
                  hp_scr, h_scr, u_scr, uprev_scr, c_scr, v_scr, rx_scr, rxprev_scr, xb_scr, xbb_scr,
                  y_scr, m_scr, mb_scr, op_scr, hcar_scr,
                  *, gate_starts):
    T, D = x_ref.shape
    L = T // SUBLANES
    d_conv = cwo_ref.shape[0]
    d_rnn = lwo_ref.shape[0]
    u_halo = cw_ref.shape[0] - 1
    rx_halo = lcw_ref.shape[0] - 1
    assert u_halo <= L and rx_halo <= L and L % NORM_ROWS == 0 and T % CONV_ROWS == 0
    off_glu_b = d_conv
    off_rx = 2 * d_conv
    off_rg = off_rx + d_rnn
    off_ga = off_rg + d_rnn
    off_gb = off_ga + D

    @pl.when(pl.program_id(1) == 0)
    def _():
        uprev_scr[...] = jnp.zeros(uprev_scr.shape, jnp.float32)
        rxprev_scr[...] = jnp.zeros(rxprev_scr.shape, jnp.float32)
        hcar_scr[...] = jnp.zeros(hcar_scr.shape, jnp.float32)

    def interleaved_rows(r, n):
        s, i0 = divmod(r, L)
        return pl.ds(SUBLANES * i0 + s, n, stride=SUBLANES)

    for r in range(0, T, NORM_ROWS):
        x = x_ref[r:r + NORM_ROWS, :]
        ms = jnp.mean(x * x, axis=-1, keepdims=True)
        h = x * lax.rsqrt(ms + EPS) * n1g_ref[...]
        for c in range(D // LANES):
            hp_scr[c, interleaved_rows(r, NORM_ROWS), :] = h[:, c * LANES:(c + 1) * LANES]
    for c in range(D // LANES):
        h_scr[:, c * LANES:(c + 1) * LANES] = hp_scr[c].astype(jnp.bfloat16)

    def in_proj(off, c0, width=MXU_COLS):
        cols = slice(off + c0, off + c0 + width)
        return _dot(h_scr[...], win_ref[:, cols]) + bin_ref[:, cols]

    def fill_halo(buf, prev, halo):
        n, width = SUBLANES * halo, buf.shape[1]
        tail = buf[T:T + n, :].reshape(halo, SUBLANES, width)
        sub = lax.broadcasted_iota(jnp.int32, tail.shape, 1)
        mixed = jnp.where(sub == SUBLANES - 1, prev[...].reshape(tail.shape), tail)
        buf[0:n, :] = pltpu.roll(mixed, 1, 1).reshape(n, width)
        prev[...] = tail.reshape(n, width)

    always = pl.program_id(1) >= 0

    def causal_conv(buf, w_ref, b_ref, emit):
        groups = CONV_ROWS // SUBLANES
        prev = None
        for l0 in range(0, buf.shape[1], LANES):
            lanes = slice(l0, l0 + LANES)
            for r in range(0, T, CONV_ROWS):
                acc = jnp.broadcast_to(b_ref[:, lanes], (groups, SUBLANES, LANES))
                if prev is not None:
                    acc = jnp.where(always, acc, prev)
                sums = [acc[0]]
                for k in range(w_ref.shape[0]):
                    row = r + SUBLANES * k
                    wk = jnp.broadcast_to(w_ref[k:k + 1, lanes], (SUBLANES, LANES))
                    if k >= CONV_LAG:
                        wk = jnp.where(always, wk, sums[k - CONV_LAG])
                    acc = acc + wk[None] * buf[row:row + CONV_ROWS, lanes].reshape(groups, SUBLANES, LANES)
                    sums.append(acc[0])
                emit(r, lanes, acc.reshape(CONV_ROWS, LANES))
                prev = acc

    u0 = SUBLANES * u_halo
    for c0 in range(0, d_conv, MXU_COLS):
        za = in_proj(0, c0)
        zb = in_proj(off_glu_b, c0)
        u_scr[u0:u0 + T, c0:c0 + MXU_COLS] = za * _sigmoid(zb)
    fill_halo(u_scr, uprev_scr, u_halo)

    def emit_conv(r, lanes, acc):
        c_scr[r:r + CONV_ROWS, lanes] = acc
    causal_conv(u_scr, cw_ref, cb_ref, emit_conv)

    for r in range(0, T, NORM_ROWS):
        cv = c_scr[r:r + NORM_ROWS, :]
        mu = jnp.mean(cv, axis=-1, keepdims=True)
        xc = cv - mu
        var = jnp.mean(xc * xc, axis=-1, keepdims=True)
        y = xc * lax.rsqrt(var + EPS) * lng_ref[...] + lnb_ref[...]
        v_scr[r:r + NORM_ROWS, :] = _silu(y).astype(jnp.bfloat16)

    for c0 in range(0, D, MXU_COLS):
        cols = slice(c0, c0 + MXU_COLS)
        ya = _dot(v_scr[...], cwo_ref[:, cols]) + cbo_ref[:, cols]
        m_scr[:, cols] = _sigmoid(in_proj(off_ga, c0)) * ya

    rx0 = SUBLANES * rx_halo
    for c0 in range(0, d_rnn, MXU_COLS):
        rx_scr[rx0:rx0 + T, c0:c0 + MXU_COLS] = in_proj(off_rx, c0)
    fill_halo(rx_scr, rxprev_scr, rx_halo)

    def emit_rx(r, lanes, acc):
        xb_scr[r:r + CONV_ROWS, lanes] = acc
        xbb_scr[r:r + CONV_ROWS, lanes] = acc.astype(jnp.bfloat16)
    causal_conv(rx_scr, lcw_ref, lcb_ref, emit_rx)

    lam = lam_ref[...]
    neg_c_sp = -LRU_C * (jnp.maximum(-lam, 0.0) + jnp.log1p(jnp.exp(-jnp.abs(lam))))

    sub = lax.broadcasted_iota(jnp.int32, (SUBLANES, MXU_COLS), 0)
    for j, ks in enumerate(gate_starts):
        c0 = j * MXU_COLS
        cols = slice(c0, c0 + MXU_COLS)
        g = _dot(xbb_scr[:, ks:ks + GATE_K], wg_ref[j])
        r_gate = _sigmoid(g[:, :MXU_COLS] + ba_ref[:, cols])
        i_gate = _sigmoid(g[:, MXU_COLS:] + bx_ref[:, cols])
        log_a = neg_c_sp[:, cols] * r_gate
        a = jnp.exp(log_a)
        mult = jnp.sqrt(-jnp.tanh(log_a) * (1.0 + a * a))
        b = mult * (i_gate * xb_scr[:, cols])

        a3 = a.reshape(L, SUBLANES, MXU_COLS)
        b3 = b.reshape(L, SUBLANES, MXU_COLS)
        hloc, prod = [b3[0]], [a3[0]]
        for i in range(1, L):
            hloc.append(a3[i] * hloc[-1] + b3[i])
            prod.append(a3[i] * prod[-1])
        ea, eb = prod[-1], hloc[-1]
        d = 1
        while d < SUBLANES:
            keep = sub >= d
            ea_sh = jnp.where(keep, pltpu.roll(ea, d, 0), 1.0)
            eb_sh = jnp.where(keep, pltpu.roll(eb, d, 0), 0.0)
            eb = eb + ea * eb_sh
            ea = ea * ea_sh
            d *= 2
        h_in = hcar_scr[:, cols]
        ends = eb + ea * h_in
        start = jnp.where(sub == 0, h_in, pltpu.roll(ends, 1, 0))
        hcar_scr[:, cols] = jnp.broadcast_to(ends[SUBLANES - 1:SUBLANES, :], (SUBLANES, MXU_COLS))
        hseq = jnp.concatenate([hl + pr * start for hl, pr in zip(hloc, prod)], axis=0)

        y_scr[:, cols] = (hseq * _gelu_tanh(in_proj(off_rg, c0))).astype(jnp.bfloat16)

    for c0 in range(0, D, MXU_COLS):
        cols = slice(c0, c0 + MXU_COLS)
        yb = _dot(y_scr[...], lwo_ref[:, cols])
        mb_scr[:, cols] = (m_scr[:, cols] + _sigmoid(in_proj(off_gb, c0)) * yb).astype(jnp.bfloat16)

    for c0 in range(0, D, MXU_COLS):
        res = _dot(mb_scr[...], wmix_ref[:, c0:c0 + MXU_COLS])
        for q in range(MXU_COLS // LANES):
            op_scr[c0 // LANES + q] = res[:, q * LANES:(q + 1) * LANES]
    for r in range(0, T, NORM_ROWS):
        for c in range(D // LANES):
            lanes = slice(c * LANES, (c + 1) * LANES)
            o_ref[r:r + NORM_ROWS, lanes] = (x_ref[r:r + NORM_ROWS, lanes]
                                             + op_scr[c, interleaved_rows(r, NORM_ROWS), :])


def _ffn_kernel(x_ref, n2g_ref, w1_ref, w3_ref, w2_ref, nfg_ref, o_ref, h_scr, act_scr, y_scr):
    tm, D = x_ref.shape
    d_ff = w1_ref.shape[1]

    for r in range(0, tm, NORM_ROWS):
        x = x_ref[r:r + NORM_ROWS, :]
        ms = jnp.mean(x * x, axis=-1, keepdims=True)
        h_scr[r:r + NORM_ROWS, :] = (x * lax.rsqrt(ms + EPS) * n2g_ref[...]).astype(jnp.bfloat16)

    for c0 in range(0, d_ff, MXU_COLS):
        cols = slice(c0, c0 + MXU_COLS)
        g = _dot(h_scr[...], w1_ref[:, cols])
        u = _dot(h_scr[...], w3_ref[:, cols])
        act_scr[:, cols] = (_silu(g) * u).astype(jnp.bfloat16)

    for c0 in range(0, D, MXU_COLS):
        cols = slice(c0, c0 + MXU_COLS)
        y_scr[:, cols] = x_ref[:, cols] + _dot(act_scr[...], w2_ref[:, cols])

    for r in range(0, tm, NORM_ROWS):
        y = y_scr[r:r + NORM_ROWS, :]
        ms = jnp.mean(y * y, axis=-1, keepdims=True)
        o_ref[r:r + NORM_ROWS, :] = y * lax.rsqrt(ms + EPS) * nfg_ref[...]


def _resident(shape):
    return pl.BlockSpec(shape, lambda *_: (0,) * len(shape), pipeline_mode=pl.Buffered(1))


def _layer_resident(shape, layer):
    return pl.BlockSpec((None,) + tuple(shape), lambda *_: (layer,) + (0,) * len(shape),
                        pipeline_mode=pl.Buffered(1))


def _stacked_rows(p):
    return p.reshape(p.shape[0], 1, p.shape[1])


def _gate_weights(wa, wx, gate_starts):
    heads, blk, _ = wa.shape
    d_rnn = heads * blk
    head_of = np.arange(d_rnn) // blk
    same_head = jnp.asarray(head_of[:, None] == head_of[None, :])

    def dense(w):
        return jnp.where(same_head, jnp.tile(w.reshape(d_rnn, blk), (1, heads)), 0.0)

    da, dx = dense(wa), dense(wx)
    tiles = []
    for j, ks in enumerate(gate_starts):
        cols = slice(j * MXU_COLS, (j + 1) * MXU_COLS)
        tiles.append(jnp.concatenate([da[ks:ks + GATE_K, cols], dx[ks:ks + GATE_K, cols]], axis=1))
    return jnp.stack(tiles).astype(jnp.bfloat16)


def kernel(x, norm1_g, w_in, b_in, conv_dw_w, conv_dw_b, conv_ln_g, conv_ln_b, conv_w_out, conv_b_out,
           lru_conv_w, lru_conv_b, lru_wa, lru_ba, lru_wx, lru_bx, lru_lambda, lru_w_out, w_mix_out,
           norm2_g, ffn_w1, ffn_w3, ffn_w2, norm_f_g):
    B, S, D = x.shape
    depth = w_in.shape[0]
    d_in = w_in.shape[2]
    conv_k, d_conv = conv_dw_w.shape[1:]
    lru_k, d_rnn = lru_conv_w.shape[1:]
    heads, blk = lru_wa.shape[1:3]
    d_ff = ffn_w1.shape[2]
    T = MIX_TILE
    assert S % T == 0 and (B * S) % FFN_TILE == 0
    assert d_in == 2 * d_conv + 2 * d_rnn + 2 * D and heads * blk == d_rnn
    assert d_conv % MXU_COLS == 0 and d_rnn % MXU_COLS == 0 and D % MXU_COLS == 0 and d_ff % MXU_COLS == 0
    u_halo_rows = SUBLANES * (conv_k - 1)
    rx_halo_rows = SUBLANES * (lru_k - 1)
    gate_starts = _gate_windows(d_rnn, blk)
    n_gate = len(gate_starts)
    bf16 = jnp.bfloat16
    f32 = jnp.float32

    def row(v):
        return v.reshape(1, -1)

    for l in range(depth):
        wg = _gate_weights(lru_wa[l], lru_wx[l], gate_starts)
        mixer = pl.pallas_call(
            functools.partial(_mixer_kernel, gate_starts=gate_starts),
            grid=(B, S // T),
            in_specs=[
                pl.BlockSpec((None, T, D), lambda b, s: (b, s, 0)),
                _layer_resident((1, D), l),
                _resident((D, d_in)), _layer_resident((1, d_in), l),
                _layer_resident((conv_k, d_conv), l), _layer_resident((1, d_conv), l),
                _layer_resident((1, d_conv), l), _layer_resident((1, d_conv), l),
                _resident((d_conv, D)), _layer_resident((1, D), l),
                _layer_resident((lru_k, d_rnn), l), _layer_resident((1, d_rnn), l),
                _resident((n_gate, GATE_K, 2 * MXU_COLS)),
                _layer_resident((1, d_rnn), l), _layer_resident((1, d_rnn), l), _layer_resident((1, d_rnn), l),
                _resident((d_rnn, D)), _resident((D, D)),
            ],
            out_specs=pl.BlockSpec((None, T, D), lambda b, s: (b, s, 0)),
            out_shape=jax.ShapeDtypeStruct((B, S, D), f32),
            scratch_shapes=[
                pltpu.VMEM((D // LANES, T, LANES), f32),
                pltpu.VMEM((T, D), bf16),
                pltpu.VMEM((u_halo_rows + T, d_conv), f32),
                pltpu.VMEM((u_halo_rows, d_conv), f32),
                pltpu.VMEM((T, d_conv), f32),
                pltpu.VMEM((T, d_conv), bf16),
                pltpu.VMEM((rx_halo_rows + T, d_rnn), f32),
                pltpu.VMEM((rx_halo_rows, d_rnn), f32),
                pltpu.VMEM((T, d_rnn), f32),
                pltpu.VMEM((T, d_rnn), bf16),
                pltpu.VMEM((T, d_rnn), bf16),
                pltpu.VMEM((T, D), f32),
                pltpu.VMEM((T, D), bf16),
                pltpu.VMEM((D // LANES, T, LANES), f32),
                pltpu.VMEM((SUBLANES, d_rnn), f32),
            ],
            compiler_params=pltpu.CompilerParams(
                dimension_semantics=("arbitrary", "arbitrary"),
                vmem_limit_bytes=MIX_VMEM_BYTES),
            name="token_mixer",
        )
        x = mixer(
            x, _stacked_rows(norm1_g), w_in[l].astype(bf16), _stacked_rows(b_in),
            conv_dw_w, _stacked_rows(conv_dw_b), _stacked_rows(conv_ln_g), _stacked_rows(conv_ln_b),
            conv_w_out[l].astype(bf16), _stacked_rows(conv_b_out),
            lru_conv_w, _stacked_rows(lru_conv_b), wg, _stacked_rows(lru_ba), _stacked_rows(lru_bx),
            _stacked_rows(lru_lambda),
            lru_w_out[l].astype(bf16), w_mix_out[l].astype(bf16))

        last = l == depth - 1
        ffn = pl.pallas_call(
            _ffn_kernel,
            grid=(B * S // FFN_TILE,),
            in_specs=[
                pl.BlockSpec((FFN_TILE, D), lambda i: (i, 0)),
                _layer_resident((1, D), l),
                _resident((D, d_ff)), _resident((D, d_ff)), _resident((d_ff, D)),
                _resident((1, D)),
            ],
            out_specs=pl.BlockSpec((FFN_TILE, D), lambda i: (i, 0)),
            out_shape=jax.ShapeDtypeStruct((B * S, D), f32),
            scratch_shapes=[
                pltpu.VMEM((FFN_TILE, D), bf16),
                pltpu.VMEM((FFN_TILE, d_ff), bf16),
                pltpu.VMEM((FFN_TILE, D), f32),
            ],
            compiler_params=pltpu.CompilerParams(
                dimension_semantics=("arbitrary",),
                vmem_limit_bytes=FFN_VMEM_BYTES),
            name="channel_mixer",
        )
        assert last, "the final rmsnorm is fused into the last layer's channel mixer"
        x = ffn(x.reshape(B * S, D), _stacked_rows(norm2_g), ffn_w1[l].astype(bf16), ffn_w3[l].astype(bf16),
                ffn_w2[l].astype(bf16), row(norm_f_g)).reshape(B, S, D)
    return x
```

```python
import functools
import math

import jax
import jax.numpy as jnp
import numpy as np
from jax import lax
from jax.experimental import pallas as pl
from jax.experimental.pallas import tpu as pltpu

EPS = 1e-6
LRU_C = 8.0

SUBLANES = 8
LANES = 128
MXU_COLS = 256

MIX_TILE = 512
FFN_TILE = 1024
CONV_ROWS = 128
CONV_LAG = 0
NORM_ROWS = 32
GATE_K = 512

MIX_VMEM_BYTES = 60 * 1024 * 1024
FFN_VMEM_BYTES = 56 * 1024 * 1024


def _sigmoid(x):
    return 0.5 * jnp.tanh(0.5 * x) + 0.5


def _silu(x):
    return x * _sigmoid(x)


def _gelu_tanh(x):
    c = math.sqrt(2.0 / math.pi)
    return 0.5 * x * (1.0 + jnp.tanh(c * (x + 0.044715 * (x * x * x))))


def _dot(a, b):
    return jnp.dot(a, b, preferred_element_type=jnp.float32)


def _gate_windows(d_rnn, block):
    starts = []
    for j in range(d_rnn // MXU_COLS):
        lo = (j * MXU_COLS // block) * block
        hi = -(-((j + 1) * MXU_COLS) // block) * block
        start = min(lo // LANES * LANES, d_rnn - GATE_K)
        assert start <= lo and hi <= start + GATE_K
        starts.append(start)
    return tuple(starts)


def _mixer_kernel(x_ref, n1g_ref, win_ref, bin_ref, cw_ref, cb_ref, lng_ref, lnb_ref, cwo_ref, cbo_ref,
                  lcw_ref, lcb_ref, wg_ref, ba_ref, bx_ref, lam_ref, lwo_ref, wmix_ref,
                  o_ref,
                  hp_scr, h_scr, u_scr, uprev_scr, c_scr, v_scr, rx_scr, rxprev_scr, xb_scr, xbb_scr,
                  y_scr, m_scr, mb_scr, op_scr, hcar_scr,
                  *, gate_starts):
    T, D = x_ref.shape
    L = T // SUBLANES
    d_conv = cwo_ref.shape[0]
    d_rnn = lwo_ref.shape[0]
    u_halo = cw_ref.shape[0] - 1
    rx_halo = lcw_ref.shape[0] - 1
    assert u_halo <= L and rx_halo <= L and L % NORM_ROWS == 0 and T % CONV_ROWS == 0
    off_glu_b = d_conv
    off_rx = 2 * d_conv
    off_rg = off_rx + d_rnn
    off_ga = off_rg + d_rnn
    off_gb = off_ga + D

    @pl.when(pl.program_id(1) == 0)
    def _():
        uprev_scr[...] = jnp.zeros(uprev_scr.shape, jnp.float32)
        rxprev_scr[...] = jnp.zeros(rxprev_scr.shape, jnp.float32)
        hcar_scr[...] = jnp.zeros(hcar_scr.shape, jnp.float32)

    def interleaved_rows(r, n):
        s, i0 = divmod(r, L)
        return pl.ds(SUBLANES * i0 + s, n, stride=SUBLANES)

    for r in range(0, T, NORM_ROWS):
        x = x_ref[r:r + NORM_ROWS, :]
        ms = jnp.mean(x * x, axis=-1, keepdims=True)
        h = x * lax.rsqrt(ms + EPS) * n1g_ref[...]
        for c in range(D // LANES):
            hp_scr[c, interleaved_rows(r, NORM_ROWS), :] = h[:, c * LANES:(c + 1) * LANES]
    for c in range(D // LANES):
        h_scr[:, c * LANES:(c + 1) * LANES] = hp_scr[c].astype(jnp.bfloat16)

    def in_proj(off, c0, width=MXU_COLS):
        cols = slice(off + c0, off + c0 + width)
        return _dot(h_scr[...], win_ref[:, cols]) + bin_ref[:, cols]

    def fill_halo(buf, prev, halo):
        n, width = SUBLANES * halo, buf.shape[1]
        tail = buf[T:T + n, :].reshape(halo, SUBLANES, width)
        sub = lax.broadcasted_iota(jnp.int32, tail.shape, 1)
        mixed = jnp.where(sub == SUBLANES - 1, prev[...].reshape(tail.shape), tail)
        buf[0:n, :] = pltpu.roll(mixed, 1, 1).reshape(n, width)
        prev[...] = tail.reshape(n, width)

    always = pl.program_id(1) >= 0

    def causal_conv(buf, w_ref, b_ref, emit):
        groups = CONV_ROWS // SUBLANES
        prev = None
        for l0 in range(0, buf.shape[1], LANES):
            lanes = slice(l0, l0 + LANES)
            for r in range(0, T, CONV_ROWS):
                acc = jnp.broadcast_to(b_ref[:, lanes], (groups, SUBLANES, LANES))
                if prev is not None:
                    acc = jnp.where(always, acc, prev)
                sums = [acc[0]]
                for k in range(w_ref.shape[0]):
                    row = r + SUBLANES * k
                    wk = jnp.broadcast_to(w_ref[k:k + 1, lanes], (SUBLANES, LANES))
                    if k >= CONV_LAG:
                        wk = jnp.where(always, wk, sums[k - CONV_LAG])
                    acc = acc + wk[None] * buf[row:row + CONV_ROWS, lanes].reshape(groups, SUBLANES, LANES)
                    sums.append(acc[0])
                emit(r, lanes, acc.reshape(CONV_ROWS, LANES))
                prev = acc

    u0 = SUBLANES * u_halo
    for c0 in range(0, d_conv, MXU_COLS):
        za = in_proj(0, c0)
        zb = in_proj(off_glu_b, c0)
        u_scr[u0:u0 + T, c0:c0 + MXU_COLS] = za * _sigmoid(zb)
    fill_halo(u_scr, uprev_scr, u_halo)

    def emit_conv(r, lanes, acc):
        c_scr[r:r + CONV_ROWS, lanes] = acc
    causal_conv(u_scr, cw_ref, cb_ref, emit_conv)

    for r in range(0, T, NORM_ROWS):
        cv = c_scr[r:r + NORM_ROWS, :]
        mu = jnp.mean(cv, axis=-1, keepdims=True)
        xc = cv - mu
        var = jnp.mean(xc * xc, axis=-1, keepdims=True)
        y = xc * lax.rsqrt(var + EPS) * lng_ref[...] + lnb_ref[...]
        v_scr[r:r + NORM_ROWS, :] = _silu(y).astype(jnp.bfloat16)

    for c0 in range(0, D, MXU_COLS):
        cols = slice(c0, c0 + MXU_COLS)
        ya = _dot(v_scr[...], cwo_ref[:, cols]) + cbo_ref[:, cols]
        m_scr[:, cols] = _sigmoid(in_proj(off_ga, c0)) * ya

    rx0 = SUBLANES * rx_halo
    for c0 in range(0, d_rnn, MXU_COLS):
        rx_scr[rx0:rx0 + T, c0:c0 + MXU_COLS] = in_proj(off_rx, c0)
    fill_halo(rx_scr, rxprev_scr, rx_halo)

    def emit_rx(r, lanes, acc):
        xb_scr[r:r + CONV_ROWS, lanes] = acc
        xbb_scr[r:r + CONV_ROWS, lanes] = acc.astype(jnp.bfloat16)
    causal_conv(rx_scr, lcw_ref, lcb_ref, emit_rx)

    lam = lam_ref[...]
    neg_c_sp = -LRU_C * (jnp.maximum(-lam, 0.0) + jnp.log1p(jnp.exp(-jnp.abs(lam))))

    sub = lax.broadcasted_iota(jnp.int32, (SUBLANES, MXU_COLS), 0)
    for j, ks in enumerate(gate_starts):
        c0 = j * MXU_COLS
        cols = slice(c0, c0 + MXU_COLS)
        g = _dot(xbb_scr[:, ks:ks + GATE_K], wg_ref[j])
        r_gate = _sigmoid(g[:, :MXU_COLS] + ba_ref[:, cols])
        i_gate = _sigmoid(g[:, MXU_COLS:] + bx_ref[:, cols])
        log_a = neg_c_sp[:, cols] * r_gate
        a = jnp.exp(log_a)
        mult = jnp.sqrt(-jnp.tanh(log_a) * (1.0 + a * a))
        b = mult * (i_gate * xb_scr[:, cols])

        a3 = a.reshape(L, SUBLANES, MXU_COLS)
        b3 = b.reshape(L, SUBLANES, MXU_COLS)
        hloc, prod = [b3[0]], [a3[0]]
        for i in range(1, L):
            hloc.append(a3[i] * hloc[-1] + b3[i])
            prod.append(a3[i] * prod[-1])
        ea, eb = prod[-1], hloc[-1]
        d = 1
        while d < SUBLANES:
            keep = sub >= d
            ea_sh = jnp.where(keep, pltpu.roll(ea, d, 0), 1.0)
            eb_sh = jnp.where(keep, pltpu.roll(eb, d, 0), 0.0)
            eb = eb + ea * eb_sh
            ea = ea * ea_sh
            d *= 2
        h_in = hcar_scr[:, cols]
        ends = eb + ea * h_in
        start = jnp.where(sub == 0, h_in, pltpu.roll(ends, 1, 0))
        hcar_scr[:, cols] = jnp.broadcast_to(ends[SUBLANES - 1:SUBLANES, :], (SUBLANES, MXU_COLS))
        hseq = jnp.concatenate([hl + pr * start for hl, pr in zip(hloc, prod)], axis=0)

        y_scr[:, cols] = (hseq * _gelu_tanh(in_proj(off_rg, c0))).astype(jnp.bfloat16)

    for c0 in range(0, D, MXU_COLS):
        cols = slice(c0, c0 + MXU_COLS)
        yb = _dot(y_scr[...], lwo_ref[:, cols])
        mb_scr[:, cols] = (m_scr[:, cols] + _sigmoid(in_proj(off_gb, c0)) * yb).astype(jnp.bfloat16)

    for c0 in range(0, D, MXU_COLS):
        res = _dot(mb_scr[...], wmix_ref[:, c0:c0 + MXU_COLS])
        for q in range(MXU_COLS // LANES):
            op_scr[c0 // LANES + q] = res[:, q * LANES:(q + 1) * LANES]
    for r in range(0, T, NORM_ROWS):
        for c in range(D // LANES):
            lanes = slice(c * LANES, (c + 1) * LANES)
            o_ref[r:r + NORM_ROWS, lanes] = (x_ref[r:r + NORM_ROWS, lanes]
                                             + op_scr[c, interleaved_rows(r, NORM_ROWS), :])


def _ffn_kernel(x_ref, n2g_ref, w1_ref, w3_ref, w2_ref, nfg_ref, o_ref, h_scr, act_scr, y_scr):
    tm, D = x_ref.shape
    d_ff = w1_ref.shape[1]

    for r in range(0, tm, NORM_ROWS):
        x = x_ref[r:r + NORM_ROWS, :]
        ms = jnp.mean(x * x, axis=-1, keepdims=True)
        h_scr[r:r + NORM_ROWS, :] = (x * lax.rsqrt(ms + EPS) * n2g_ref[...]).astype(jnp.bfloat16)

    for c0 in range(0, d_ff, MXU_COLS):
        cols = slice(c0, c0 + MXU_COLS)
        g = _dot(h_scr[...], w1_ref[:, cols])
        u = _dot(h_scr[...], w3_ref[:, cols])
        act_scr[:, cols] = (_silu(g) * u).astype(jnp.bfloat16)

    for c0 in range(0, D, MXU_COLS):
        cols = slice(c0, c0 + MXU_COLS)
        y_scr[:, cols] = x_ref[:, cols] + _dot(act_scr[...], w2_ref[:, cols])

    for r in range(0, tm, NORM_ROWS):
        y = y_scr[r:r + NORM_ROWS, :]
        ms = jnp.mean(y * y, axis=-1, keepdims=True)
        o_ref[r:r + NORM_ROWS, :] = y * lax.rsqrt(ms + EPS) * nfg_ref[...]


def _resident(shape):
    return pl.BlockSpec(shape, lambda *_: (0,) * len(shape), pipeline_mode=pl.Buffered(1))


def _layer_resident(shape, layer):
    return pl.BlockSpec((None,) + tuple(shape), lambda *_: (layer,) + (0,) * len(shape),
                        pipeline_mode=pl.Buffered(1))


def _stacked_rows(p):
    return p.reshape(p.shape[0], 1, p.shape[1])


def _gate_weights(wa, wx, gate_starts):
    heads, blk, _ = wa.shape
    d_rnn = heads * blk
    head_of = np.arange(d_rnn) // blk
    same_head = jnp.asarray(head_of[:, None] == head_of[None, :])

    def dense(w):
        return jnp.where(same_head, jnp.tile(w.reshape(d_rnn, blk), (1, heads)), 0.0)

    da, dx = dense(wa), dense(wx)
    tiles = []
    for j, ks in enumerate(gate_starts):
        cols = slice(j * MXU_COLS, (j + 1) * MXU_COLS)
        tiles.append(jnp.concatenate([da[ks:ks + GATE_K, cols], dx[ks:ks + GATE_K, cols]], axis=1))
    return jnp.stack(tiles).astype(jnp.bfloat16)


def kernel(x, norm1_g, w_in, b_in, conv_dw_w, conv_dw_b, conv_ln_g, conv_ln_b, conv_w_out, conv_b_out,
           lru_conv_w, lru_conv_b, lru_wa, lru_ba, lru_wx, lru_bx, lru_lambda, lru_w_out, w_mix_out,
           norm2_g, ffn_w1, ffn_w3, ffn_w2, norm_f_g):
    B, S, D = x.shape
    depth = w_in.shape[0]
    d_in = w_in.shape[2]
    conv_k, d_conv = conv_dw_w.shape[1:]
    lru_k, d_rnn = lru_conv_w.shape[1:]
    heads, blk = lru_wa.shape[1:3]
    d_ff = ffn_w1.shape[2]
    T = MIX_TILE
    assert S % T == 0 and (B * S) % FFN_TILE == 0
    assert d_in == 2 * d_conv + 2 * d_rnn + 2 * D and heads * blk == d_rnn
    assert d_conv % MXU_COLS == 0 and d_rnn % MXU_COLS == 0 and D % MXU_COLS == 0 and d_ff % MXU_COLS == 0
    u_halo_rows = SUBLANES * (conv_k - 1)
    rx_halo_rows = SUBLANES * (lru_k - 1)
    gate_starts = _gate_windows(d_rnn, blk)
    n_gate = len(gate_starts)
    bf16 = jnp.bfloat16
    f32 = jnp.float32

    def row(v):
        return v.reshape(1, -1)

    for l in range(depth):
        wg = _gate_weights(lru_wa[l], lru_wx[l], gate_starts)
        mixer = pl.pallas_call(
            functools.partial(_mixer_kernel, gate_starts=gate_starts),
            grid=(B, S // T),
            in_specs=[
                pl.BlockSpec((None, T, D), lambda b, s: (b, s, 0)),
                _layer_resident((1, D), l),
                _resident((D, d_in)), _layer_resident((1, d_in), l),
                _layer_resident((conv_k, d_conv), l), _layer_resident((1, d_conv), l),
                _layer_resident((1, d_conv), l), _layer_resident((1, d_conv), l),
                _resident((d_conv, D)), _layer_resident((1, D), l),
                _layer_resident((lru_k, d_rnn), l), _layer_resident((1, d_rnn), l),
                _resident((n_gate, GATE_K, 2 * MXU_COLS)),
                _layer_resident((1, d_rnn), l), _layer_resident((1, d_rnn), l), _layer_resident((1, d_rnn), l),
                _resident((d_rnn, D)), _resident((D, D)),
            ],
            out_specs=pl.BlockSpec((None, T, D), lambda b, s: (b, s, 0)),
            out_shape=jax.ShapeDtypeStruct((B, S, D), f32),
            scratch_shapes=[
                pltpu.VMEM((D // LANES, T, LANES), f32),
                pltpu.VMEM((T, D), bf16),
                pltpu.VMEM((u_halo_rows + T, d_conv), f32),
                pltpu.VMEM((u_halo_rows, d_conv), f32),
                pltpu.VMEM((T, d_conv), f32),
                pltpu.VMEM((T, d_conv), bf16),
                pltpu.VMEM((rx_halo_rows + T, d_rnn), f32),
                pltpu.VMEM((rx_halo_rows, d_rnn), f32),
                pltpu.VMEM((T, d_rnn), f32),
                pltpu.VMEM((T, d_rnn), bf16),
                pltpu.VMEM((T, d_rnn), bf16),
                pltpu.VMEM((T, D), f32),
                pltpu.VMEM((T, D), bf16),
                pltpu.VMEM((D // LANES, T, LANES), f32),
                pltpu.VMEM((SUBLANES, d_rnn), f32),
            ],
            compiler_params=pltpu.CompilerParams(
                dimension_semantics=("arbitrary", "arbitrary"),
                vmem_limit_bytes=MIX_VMEM_BYTES),
            name="token_mixer",
        )
        x = mixer(
            x, _stacked_rows(norm1_g), w_in[l].astype(bf16), _stacked_rows(b_in),
            conv_dw_w, _stacked_rows(conv_dw_b), _stacked_rows(conv_ln_g), _stacked_rows(conv_ln_b),
            conv_w_out[l].astype(bf16), _stacked_rows(conv_b_out),
            lru_conv_w, _stacked_rows(lru_conv_b), wg, _stacked_rows(lru_ba), _stacked_rows(lru_bx),
            _stacked_rows(lru_lambda),
            lru_w_out[l].astype(bf16), w_mix_out[l].astype(bf16))

        last = l == depth - 1
        ffn = pl.pallas_call(
            _ffn_kernel,
            grid=(B * S // FFN_TILE,),
            in_specs=[
                pl.BlockSpec((FFN_TILE, D), lambda i: (i, 0)),
                _layer_resident((1, D), l),
                _resident((D, d_ff)), _resident((D, d_ff)), _resident((d_ff, D)),
                _resident((1, D)),
            ],
            out_specs=pl.BlockSpec((FFN_TILE, D), lambda i: (i, 0)),
            out_shape=jax.ShapeDtypeStruct((B * S, D), f32),
            scratch_shapes=[
                pltpu.VMEM((FFN_TILE, D), bf16),
                pltpu.VMEM((FFN_TILE, d_ff), bf16),
                pltpu.VMEM((FFN_TILE, D), f32),
            ],
            compiler_params=pltpu.CompilerParams(
                dimension_semantics=("arbitrary",),
                vmem_limit_bytes=FFN_VMEM_BYTES),
            name="channel_mixer",
        )
        assert last, "the final rmsnorm is fused into the last layer's channel mixer"
        x = ffn(x.reshape(B * S, D), _stacked_rows(norm2_g), ffn_w1[l].astype(bf16), ffn_w3[l].astype(bf16),
                ffn_w2[l].astype(bf16), row(norm_f_g)).reshape(B, S, D)
    return x
```

```python
import functools
import math

import jax
import jax.numpy as jnp
import numpy as np
from jax import lax
from jax.experimental import pallas as pl
from jax.experimental.pallas import tpu as pltpu

EPS = 1e-6
LRU_C = 8.0

SUBLANES = 8
LANES = 128
MXU_COLS = 256

MIX_TILE = 512
FFN_TILE = 1024
CONV_ROWS = 128
CONV_LAG = 2
NORM_ROWS = 32
GATE_K = 512

MIX_VMEM_BYTES = 60 * 1024 * 1024
FFN_VMEM_BYTES = 56 * 1024 * 1024


def _sigmoid(x):
    return 0.5 * jnp.tanh(0.5 * x) + 0.5


def _silu(x):
    return x * _sigmoid(x)


def _gelu_tanh(x):
    c = math.sqrt(2.0 / math.pi)
    return 0.5 * x * (1.0 + jnp.tanh(c * (x + 0.044715 * (x * x * x))))


def _dot(a, b):
    return jnp.dot(a, b, preferred_element_type=jnp.float32)


def _gate_windows(d_rnn, block):
    starts = []
    for j in range(d_rnn // MXU_COLS):
        lo = (j * MXU_COLS // block) * block
        hi = -(-((j + 1) * MXU_COLS) // block) * block
        start = min(lo // LANES * LANES, d_rnn - GATE_K)
        assert start <= lo and hi <= start + GATE_K
        starts.append(start)
    return tuple(starts)


def _mixer_kernel(x_ref, n1g_ref, win_ref, bin_ref, cw_ref, cb_ref, lng_ref, lnb_ref, cwo_ref, cbo_ref,
                  lcw_ref, lcb_ref, wg_ref, ba_ref, bx_ref, lam_ref, lwo_ref, wmix_ref,
                  o_ref,
                  hp_scr, h_scr, u_scr, uprev_scr, c_scr, v_scr, rx_scr, rxprev_scr, xb_scr, xbb_scr,
                  y_scr, m_scr, mb_scr, op_scr, hcar_scr,
                  *, gate_starts):
    T, D = x_ref.shape
    L = T // SUBLANES
    d_conv = cwo_ref.shape[0]
    d_rnn = lwo_ref.shape[0]
    u_halo = cw_ref.shape[0] - 1
    rx_halo = lcw_ref.shape[0] - 1
    assert u_halo <= L and rx_halo <= L and L % NORM_ROWS == 0 and T % CONV_ROWS == 0
    off_glu_b = d_conv
    off_rx = 2 * d_conv
    off_rg = off_rx + d_rnn
    off_ga = off_rg + d_rnn
    off_gb = off_ga + D

    @pl.when(pl.program_id(1) == 0)
    def _():
        uprev_scr[...] = jnp.zeros(uprev_scr.shape, jnp.float32)
        rxprev_scr[...] = jnp.zeros(rxprev_scr.shape, jnp.float32)
        hcar_scr[...] = jnp.zeros(hcar_scr.shape, jnp.float32)

    def interleaved_rows(r, n):
        s, i0 = divmod(r, L)
        return pl.ds(SUBLANES * i0 + s, n, stride=SUBLANES)

    for r in range(0, T, NORM_ROWS):
        x = x_ref[r:r + NORM_ROWS, :]
        ms = jnp.mean(x * x, axis=-1, keepdims=True)
        h = x * lax.rsqrt(ms + EPS) * n1g_ref[...]
        for c in range(D // LANES):
            hp_scr[c, interleaved_rows(r, NORM_ROWS), :] = h[:, c * LANES:(c + 1) * LANES]
    for c in range(D // LANES):
        h_scr[:, c * LANES:(c + 1) * LANES] = hp_scr[c].astype(jnp.bfloat16)

    def in_proj(off, c0, width=MXU_COLS):
        cols = slice(off + c0, off + c0 + width)
        return _dot(h_scr[...], win_ref[:, cols]) + bin_ref[:, cols]

    def fill_halo(buf, prev, halo):
        n, width = SUBLANES * halo, buf.shape[1]
        tail = buf[T:T + n, :].reshape(halo, SUBLANES, width)
        sub = lax.broadcasted_iota(jnp.int32, tail.shape, 1)
        mixed = jnp.where(sub == SUBLANES - 1, prev[...].reshape(tail.shape), tail)
        buf[0:n, :] = pltpu.roll(mixed, 1, 1).reshape(n, width)
        prev[...] = tail.reshape(n, width)

    always = pl.program_id(1) >= 0

    def causal_conv(buf, w_ref, b_ref, emit):
        groups = CONV_ROWS // SUBLANES
        prev = None
        for l0 in range(0, buf.shape[1], LANES):
            lanes = slice(l0, l0 + LANES)
            for r in range(0, T, CONV_ROWS):
                acc = jnp.broadcast_to(b_ref[:, lanes], (groups, SUBLANES, LANES))
                if prev is not None:
                    acc = jnp.where(always, acc, prev)
                sums = [acc[0]]
                for k in range(w_ref.shape[0]):
                    row = r + SUBLANES * k
                    wk = jnp.broadcast_to(w_ref[k:k + 1, lanes], (SUBLANES, LANES))
                    if k >= CONV_LAG:
                        wk = jnp.where(always, wk, sums[k - CONV_LAG])
                    acc = acc + wk[None] * buf[row:row + CONV_ROWS, lanes].reshape(groups, SUBLANES, LANES)
                    sums.append(acc[0])
                emit(r, lanes, acc.reshape(CONV_ROWS, LANES))
                prev = acc

    u0 = SUBLANES * u_halo
    for c0 in range(0, d_conv, MXU_COLS):
        za = in_proj(0, c0)
        zb = in_proj(off_glu_b, c0)
        u_scr[u0:u0 + T, c0:c0 + MXU_COLS] = za * _sigmoid(zb)
    fill_halo(u_scr, uprev_scr, u_halo)

    def emit_conv(r, lanes, acc):
        c_scr[r:r + CONV_ROWS, lanes] = acc
    causal_conv(u_scr, cw_ref, cb_ref, emit_conv)

    for r in range(0, T, NORM_ROWS):
        cv = c_scr[r:r + NORM_ROWS, :]
        mu = jnp.mean(cv, axis=-1, keepdims=True)
        xc = cv - mu
        var = jnp.mean(xc * xc, axis=-1, keepdims=True)
        y = xc * lax.rsqrt(var + EPS) * lng_ref[...] + lnb_ref[...]
        v_scr[r:r + NORM_ROWS, :] = _silu(y).astype(jnp.bfloat16)

    for c0 in range(0, D, MXU_COLS):
        cols = slice(c0, c0 + MXU_COLS)
        ya = _dot(v_scr[...], cwo_ref[:, cols]) + cbo_ref[:, cols]
        m_scr[:, cols] = _sigmoid(in_proj(off_ga, c0)) * ya

    rx0 = SUBLANES * rx_halo
    for c0 in range(0, d_rnn, MXU_COLS):
        rx_scr[rx0:rx0 + T, c0:c0 + MXU_COLS] = in_proj(off_rx, c0)
    fill_halo(rx_scr, rxprev_scr, rx_halo)

    def emit_rx(r, lanes, acc):
        xb_scr[r:r + CONV_ROWS, lanes] = acc
        xbb_scr[r:r + CONV_ROWS, lanes] = acc.astype(jnp.bfloat16)
    causal_conv(rx_scr, lcw_ref, lcb_ref, emit_rx)

    lam = lam_ref[...]
    neg_c_sp = -LRU_C * (jnp.maximum(-lam, 0.0) + jnp.log1p(jnp.exp(-jnp.abs(lam))))

    sub = lax.broadcasted_iota(jnp.int32, (SUBLANES, MXU_COLS), 0)
    for j, ks in enumerate(gate_starts):
        c0 = j * MXU_COLS
        cols = slice(c0, c0 + MXU_COLS)
        g = _dot(xbb_scr[:, ks:ks + GATE_K], wg_ref[j])
        r_gate = _sigmoid(g[:, :MXU_COLS] + ba_ref[:, cols])
        i_gate = _sigmoid(g[:, MXU_COLS:] + bx_ref[:, cols])
        log_a = neg_c_sp[:, cols] * r_gate
        a = jnp.exp(log_a)
        mult = jnp.sqrt(-jnp.tanh(log_a) * (1.0 + a * a))
        b = mult * (i_gate * xb_scr[:, cols])

        a3 = a.reshape(L, SUBLANES, MXU_COLS)
        b3 = b.reshape(L, SUBLANES, MXU_COLS)
        hloc, prod = [b3[0]], [a3[0]]
        for i in range(1, L):
            hloc.append(a3[i] * hloc[-1] + b3[i])
            prod.append(a3[i] * prod[-1])
        ea, eb = prod[-1], hloc[-1]
        d = 1
        while d < SUBLANES:
            keep = sub >= d
            ea_sh = jnp.where(keep, pltpu.roll(ea, d, 0), 1.0)
            eb_sh = jnp.where(keep, pltpu.roll(eb, d, 0), 0.0)
            eb = eb + ea * eb_sh
            ea = ea * ea_sh
            d *= 2
        h_in = hcar_scr[:, cols]
        ends = eb + ea * h_in
        start = jnp.where(sub == 0, h_in, pltpu.roll(ends, 1, 0))
        hcar_scr[:, cols] = jnp.broadcast_to(ends[SUBLANES - 1:SUBLANES, :], (SUBLANES, MXU_COLS))
        hseq = jnp.concatenate([hl + pr * start for hl, pr in zip(hloc, prod)], axis=0)

        y_scr[:, cols] = (hseq * _gelu_tanh(in_proj(off_rg, c0))).astype(jnp.bfloat16)

    for c0 in range(0, D, MXU_COLS):
        cols = slice(c0, c0 + MXU_COLS)
        yb = _dot(y_scr[...], lwo_ref[:, cols])
        mb_scr[:, cols] = (m_scr[:, cols] + _sigmoid(in_proj(off_gb, c0)) * yb).astype(jnp.bfloat16)

    for c0 in range(0, D, MXU_COLS):
        res = _dot(mb_scr[...], wmix_ref[:, c0:c0 + MXU_COLS])
        for q in range(MXU_COLS // LANES):
            op_scr[c0 // LANES + q] = res[:, q * LANES:(q + 1) * LANES]
    for r in range(0, T, NORM_ROWS):
        for c in range(D // LANES):
            lanes = slice(c * LANES, (c + 1) * LANES)
            o_ref[r:r + NORM_ROWS, lanes] = (x_ref[r:r + NORM_ROWS, lanes]
                                             + op_scr[c, interleaved_rows(r, NORM_ROWS), :])


def _ffn_kernel(x_ref, n2g_ref, w1_ref, w3_ref, w2_ref, nfg_ref, o_ref, h_scr, act_scr, y_scr):
    tm, D = x_ref.shape
    d_ff = w1_ref.shape[1]

    for r in range(0, tm, NORM_ROWS):
        x = x_ref[r:r + NORM_ROWS, :]
        ms = jnp.mean(x * x, axis=-1, keepdims=True)
        h_scr[r:r + NORM_ROWS, :] = (x * lax.rsqrt(ms + EPS) * n2g_ref[...]).astype(jnp.bfloat16)

    for c0 in range(0, d_ff, MXU_COLS):
        cols = slice(c0, c0 + MXU_COLS)
        g = _dot(h_scr[...], w1_ref[:, cols])
        u = _dot(h_scr[...], w3_ref[:, cols])
        act_scr[:, cols] = (_silu(g) * u).astype(jnp.bfloat16)

    for c0 in range(0, D, MXU_COLS):
        cols = slice(c0, c0 + MXU_COLS)
        y_scr[:, cols] = x_ref[:, cols] + _dot(act_scr[...], w2_ref[:, cols])

    for r in range(0, tm, NORM_ROWS):
        y = y_scr[r:r + NORM_ROWS, :]
        ms = jnp.mean(y * y, axis=-1, keepdims=True)
        o_ref[r:r + NORM_ROWS, :] = y * lax.rsqrt(ms + EPS) * nfg_ref[...]


def _resident(shape):
    return pl.BlockSpec(shape, lambda *_: (0,) * len(shape), pipeline_mode=pl.Buffered(1))


def _layer_resident(shape, layer):
    return pl.BlockSpec((None,) + tuple(shape), lambda *_: (layer,) + (0,) * len(shape),
                        pipeline_mode=pl.Buffered(1))


def _stacked_rows(p):
    return p.reshape(p.shape[0], 1, p.shape[1])


def _gate_weights(wa, wx, gate_starts):
    heads, blk, _ = wa.shape
    d_rnn = heads * blk
    head_of = np.arange(d_rnn) // blk
    same_head = jnp.asarray(head_of[:, None] == head_of[None, :])

    def dense(w):
        return jnp.where(same_head, jnp.tile(w.reshape(d_rnn, blk), (1, heads)), 0.0)

    da, dx = dense(wa), dense(wx)
    tiles = []
    for j, ks in enumerate(gate_starts):
        cols = slice(j * MXU_COLS, (j + 1) * MXU_COLS)
        tiles.append(jnp.concatenate([da[ks:ks + GATE_K, cols], dx[ks:ks + GATE_K, cols]], axis=1))
    return jnp.stack(tiles).astype(jnp.bfloat16)


def kernel(x, norm1_g, w_in, b_in, conv_dw_w, conv_dw_b, conv_ln_g, conv_ln_b, conv_w_out, conv_b_out,
           lru_conv_w, lru_conv_b, lru_wa, lru_ba, lru_wx, lru_bx, lru_lambda, lru_w_out, w_mix_out,
           norm2_g, ffn_w1, ffn_w3, ffn_w2, norm_f_g):
    B, S, D = x.shape
    depth = w_in.shape[0]
    d_in = w_in.shape[2]
    conv_k, d_conv = conv_dw_w.shape[1:]
    lru_k, d_rnn = lru_conv_w.shape[1:]
    heads, blk = lru_wa.shape[1:3]
    d_ff = ffn_w1.shape[2]
    T = MIX_TILE
    assert S % T == 0 and (B * S) % FFN_TILE == 0
    assert d_in == 2 * d_conv + 2 * d_rnn + 2 * D and heads * blk == d_rnn
    assert d_conv % MXU_COLS == 0 and d_rnn % MXU_COLS == 0 and D % MXU_COLS == 0 and d_ff % MXU_COLS == 0
    u_halo_rows = SUBLANES * (conv_k - 1)
    rx_halo_rows = SUBLANES * (lru_k - 1)
    gate_starts = _gate_windows(d_rnn, blk)
    n_gate = len(gate_starts)
    bf16 = jnp.bfloat16
    f32 = jnp.float32

    def row(v):
        return v.reshape(1, -1)

    for l in range(depth):
        wg = _gate_weights(lru_wa[l], lru_wx[l], gate_starts)
        mixer = pl.pallas_call(
            functools.partial(_mixer_kernel, gate_starts=gate_starts),
            grid=(B, S // T),
            in_specs=[
                pl.BlockSpec((None, T, D), lambda b, s: (b, s, 0)),
                _layer_resident((1, D), l),
                _resident((D, d_in)), _layer_resident((1, d_in), l),
                _layer_resident((conv_k, d_conv), l), _layer_resident((1, d_conv), l),
                _layer_resident((1, d_conv), l), _layer_resident((1, d_conv), l),
                _resident((d_conv, D)), _layer_resident((1, D), l),
                _layer_resident((lru_k, d_rnn), l), _layer_resident((1, d_rnn), l),
                _resident((n_gate, GATE_K, 2 * MXU_COLS)),
                _layer_resident((1, d_rnn), l), _layer_resident((1, d_rnn), l), _layer_resident((1, d_rnn), l),
                _resident((d_rnn, D)), _resident((D, D)),
            ],
            out_specs=pl.BlockSpec((None, T, D), lambda b, s: (b, s, 0)),
            out_shape=jax.ShapeDtypeStruct((B, S, D), f32),
            scratch_shapes=[
                pltpu.VMEM((D // LANES, T, LANES), f32),
                pltpu.VMEM((T, D), bf16),
                pltpu.VMEM((u_halo_rows + T, d_conv), f32),
                pltpu.VMEM((u_halo_rows, d_conv), f32),
                pltpu.VMEM((T, d_conv), f32),
                pltpu.VMEM((T, d_conv), bf16),
                pltpu.VMEM((rx_halo_rows + T, d_rnn), f32),
                pltpu.VMEM((rx_halo_rows, d_rnn), f32),
                pltpu.VMEM((T, d_rnn), f32),
                pltpu.VMEM((T, d_rnn), bf16),
                pltpu.VMEM((T, d_rnn), bf16),
                pltpu.VMEM((T, D), f32),
                pltpu.VMEM((T, D), bf16),
                pltpu.VMEM((D // LANES, T, LANES), f32),
                pltpu.VMEM((SUBLANES, d_rnn), f32),
            ],
            compiler_params=pltpu.CompilerParams(
                dimension_semantics=("arbitrary", "arbitrary"),
                vmem_limit_bytes=MIX_VMEM_BYTES),
            name="token_mixer",
        )
        x = mixer(
            x, _stacked_rows(norm1_g), w_in[l].astype(bf16), _stacked_rows(b_in),
            conv_dw_w, _stacked_rows(conv_dw_b), _stacked_rows(conv_ln_g), _stacked_rows(conv_ln_b),
            conv_w_out[l].astype(bf16), _stacked_rows(conv_b_out),
            lru_conv_w, _stacked_rows(lru_conv_b), wg, _stacked_rows(lru_ba), _stacked_rows(lru_bx),
            _stacked_rows(lru_lambda),
            lru_w_out[l].astype(bf16), w_mix_out[l].astype(bf16))

        last = l == depth - 1
        ffn = pl.pallas_call(
            _ffn_kernel,
            grid=(B * S // FFN_TILE,),
            in_specs=[
                pl.BlockSpec((FFN_TILE, D), lambda i: (i, 0)),
                _layer_resident((1, D), l),
                _resident((D, d_ff)), _resident((D, d_ff)), _resident((d_ff, D)),
                _resident((1, D)),
            ],
            out_specs=pl.BlockSpec((FFN_TILE, D), lambda i: (i, 0)),
            out_shape=jax.ShapeDtypeStruct((B * S, D), f32),
            scratch_shapes=[
                pltpu.VMEM((FFN_TILE, D), bf16),
                pltpu.VMEM((FFN_TILE, d_ff), bf16),
                pltpu.VMEM((FFN_TILE, D), f32),
            ],
            compiler_params=pltpu.CompilerParams(
                dimension_semantics=("arbitrary",),
                vmem_limit_bytes=FFN_VMEM_BYTES),
            name="channel_mixer",
        )
        assert last, "the final rmsnorm is fused into the last layer's channel mixer"
        x = ffn(x.reshape(B * S, D), _stacked_rows(norm2_g), ffn_w1[l].astype(bf16), ffn_w3[l].astype(bf16),
                ffn_w2[l].astype(bf16), row(norm_f_g)).reshape(B, S, D)
    return x
```

```python
import functools
import math

import jax
import jax.numpy as jnp
import numpy as np
from jax import lax
from jax.experimental import pallas as pl
from jax.experimental.pallas import tpu as pltpu

EPS = 1e-6
LRU_C = 8.0

SUBLANES = 8
LANES = 128
MXU_COLS = 256

MIX_TILE = 512
FFN_TILE = 1024
CONV_ROWS = 256
CONV_LAG = 1
NORM_ROWS = 32
GATE_K = 512

MIX_VMEM_BYTES = 60 * 1024 * 1024
FFN_VMEM_BYTES = 56 * 1024 * 1024


def _sigmoid(x):
    return 0.5 * jnp.tanh(0.5 * x) + 0.5


def _silu(x):
    return x * _sigmoid(x)


def _gelu_tanh(x):
    c = math.sqrt(2.0 / math.pi)
    return 0.5 * x * (1.0 + jnp.tanh(c * (x + 0.044715 * (x * x * x))))


def _dot(a, b):
    return jnp.dot(a, b, preferred_element_type=jnp.float32)


def _gate_windows(d_rnn, block):
    starts = []
    for j in range(d_rnn // MXU_COLS):
        lo = (j * MXU_COLS // block) * block
        hi = -(-((j + 1) * MXU_COLS) // block) * block
        start = min(lo // LANES * LANES, d_rnn - GATE_K)
        assert start <= lo and hi <= start + GATE_K
        starts.append(start)
    return tuple(starts)


def _mixer_kernel(x_ref, n1g_ref, win_ref, bin_ref, cw_ref, cb_ref, lng_ref, lnb_ref, cwo_ref, cbo_ref,
                  lcw_ref, lcb_ref, wg_ref, ba_ref, bx_ref, lam_ref, lwo_ref, wmix_ref,
                  o_ref,
                  hp_scr, h_scr, u_scr, uprev_scr, c_scr, v_scr, rx_scr, rxprev_scr, xb_scr, xbb_scr,
                  y_scr, m_scr, mb_scr, op_scr, hcar_scr,
                  *, gate_starts):
    T, D = x_ref.shape
    L = T // SUBLANES
    d_conv = cwo_ref.shape[0]
    d_rnn = lwo_ref.shape[0]
    u_halo = cw_ref.shape[0] - 1
    rx_halo = lcw_ref.shape[0] - 1
    assert u_halo <= L and rx_halo <= L and L % NORM_ROWS == 0 and T % CONV_ROWS == 0
    off_glu_b = d_conv
    off_rx = 2 * d_conv
    off_rg = off_rx + d_rnn
    off_ga = off_rg + d_rnn
    off_gb = off_ga + D

    @pl.when(pl.program_id(1) == 0)
    def _():
        uprev_scr[...] = jnp.zeros(uprev_scr.shape, jnp.float32)
        rxprev_scr[...] = jnp.zeros(rxprev_scr.shape, jnp.float32)
        hcar_scr[...] = jnp.zeros(hcar_scr.shape, jnp.float32)

    def interleaved_rows(r, n):
        s, i0 = divmod(r, L)
        return pl.ds(SUBLANES * i0 + s, n, stride=SUBLANES)

    for r in range(0, T, NORM_ROWS):
        x = x_ref[r:r + NORM_ROWS, :]
        ms = jnp.mean(x * x, axis=-1, keepdims=True)
        h = x * lax.rsqrt(ms + EPS) * n1g_ref[...]
        for c in range(D // LANES):
            hp_scr[c, interleaved_rows(r, NORM_ROWS), :] = h[:, c * LANES:(c + 1) * LANES]
    for c in range(D // LANES):
        h_scr[:, c * LANES:(c + 1) * LANES] = hp_scr[c].astype(jnp.bfloat16)

    def in_proj(off, c0, width=MXU_COLS):
        cols = slice(off + c0, off + c0 + width)
        return _dot(h_scr[...], win_ref[:, cols]) + bin_ref[:, cols]

    def fill_halo(buf, prev, halo):
        n, width = SUBLANES * halo, buf.shape[1]
        tail = buf[T:T + n, :].reshape(halo, SUBLANES, width)
        sub = lax.broadcasted_iota(jnp.int32, tail.shape, 1)
        mixed = jnp.where(sub == SUBLANES - 1, prev[...].reshape(tail.shape), tail)
        buf[0:n, :] = pltpu.roll(mixed, 1, 1).reshape(n, width)
        prev[...] = tail.reshape(n, width)

    always = pl.program_id(1) >= 0

    def causal_conv(buf, w_ref, b_ref, emit):
        groups = CONV_ROWS // SUBLANES
        prev = None
        for l0 in range(0, buf.shape[1], LANES):
            lanes = slice(l0, l0 + LANES)
            for r in range(0, T, CONV_ROWS):
                acc = jnp.broadcast_to(b_ref[:, lanes], (groups, SUBLANES, LANES))
                if prev is not None:
                    acc = jnp.where(always, acc, prev)
                sums = [acc[0]]
                for k in range(w_ref.shape[0]):
                    row = r + SUBLANES * k
                    wk = jnp.broadcast_to(w_ref[k:k + 1, lanes], (SUBLANES, LANES))
                    if k >= CONV_LAG:
                        wk = jnp.where(always, wk, sums[k - CONV_LAG])
                    acc = acc + wk[None] * buf[row:row + CONV_ROWS, lanes].reshape(groups, SUBLANES, LANES)
                    sums.append(acc[0])
                emit(r, lanes, acc.reshape(CONV_ROWS, LANES))
                prev = acc

    u0 = SUBLANES * u_halo
    for c0 in range(0, d_conv, MXU_COLS):
        za = in_proj(0, c0)
        zb = in_proj(off_glu_b, c0)
        u_scr[u0:u0 + T, c0:c0 + MXU_COLS] = za * _sigmoid(zb)
    fill_halo(u_scr, uprev_scr, u_halo)

    def emit_conv(r, lanes, acc):
        c_scr[r:r + CONV_ROWS, lanes] = acc
    causal_conv(u_scr, cw_ref, cb_ref, emit_conv)

    for r in range(0, T, NORM_ROWS):
        cv = c_scr[r:r + NORM_ROWS, :]
        mu = jnp.mean(cv, axis=-1, keepdims=True)
        xc = cv - mu
        var = jnp.mean(xc * xc, axis=-1, keepdims=True)
        y = xc * lax.rsqrt(var + EPS) * lng_ref[...] + lnb_ref[...]
        v_scr[r:r + NORM_ROWS, :] = _silu(y).astype(jnp.bfloat16)

    for c0 in range(0, D, MXU_COLS):
        cols = slice(c0, c0 + MXU_COLS)
        ya = _dot(v_scr[...], cwo_ref[:, cols]) + cbo_ref[:, cols]
        m_scr[:, cols] = _sigmoid(in_proj(off_ga, c0)) * ya

    rx0 = SUBLANES * rx_halo
    for c0 in range(0, d_rnn, MXU_COLS):
        rx_scr[rx0:rx0 + T, c0:c0 + MXU_COLS] = in_proj(off_rx, c0)
    fill_halo(rx_scr, rxprev_scr, rx_halo)

    def emit_rx(r, lanes, acc):
        xb_scr[r:r + CONV_ROWS, lanes] = acc
        xbb_scr[r:r + CONV_ROWS, lanes] = acc.astype(jnp.bfloat16)
    causal_conv(rx_scr, lcw_ref, lcb_ref, emit_rx)

    lam = lam_ref[...]
    neg_c_sp = -LRU_C * (jnp.maximum(-lam, 0.0) + jnp.log1p(jnp.exp(-jnp.abs(lam))))

    sub = lax.broadcasted_iota(jnp.int32, (SUBLANES, MXU_COLS), 0)
    for j, ks in enumerate(gate_starts):
        c0 = j * MXU_COLS
        cols = slice(c0, c0 + MXU_COLS)
        g = _dot(xbb_scr[:, ks:ks + GATE_K], wg_ref[j])
        r_gate = _sigmoid(g[:, :MXU_COLS] + ba_ref[:, cols])
        i_gate = _sigmoid(g[:, MXU_COLS:] + bx_ref[:, cols])
        log_a = neg_c_sp[:, cols] * r_gate
        a = jnp.exp(log_a)
        mult = jnp.sqrt(-jnp.tanh(log_a) * (1.0 + a * a))
        b = mult * (i_gate * xb_scr[:, cols])

        a3 = a.reshape(L, SUBLANES, MXU_COLS)
        b3 = b.reshape(L, SUBLANES, MXU_COLS)
        hloc, prod = [b3[0]], [a3[0]]
        for i in range(1, L):
            hloc.append(a3[i] * hloc[-1] + b3[i])
            prod.append(a3[i] * prod[-1])
        ea, eb = prod[-1], hloc[-1]
        d = 1
        while d < SUBLANES:
            keep = sub >= d
            ea_sh = jnp.where(keep, pltpu.roll(ea, d, 0), 1.0)
            eb_sh = jnp.where(keep, pltpu.roll(eb, d, 0), 0.0)
            eb = eb + ea * eb_sh
            ea = ea * ea_sh
            d *= 2
        h_in = hcar_scr[:, cols]
        ends = eb + ea * h_in
        start = jnp.where(sub == 0, h_in, pltpu.roll(ends, 1, 0))
        hcar_scr[:, cols] = jnp.broadcast_to(ends[SUBLANES - 1:SUBLANES, :], (SUBLANES, MXU_COLS))
        hseq = jnp.concatenate([hl + pr * start for hl, pr in zip(hloc, prod)], axis=0)

        y_scr[:, cols] = (hseq * _gelu_tanh(in_proj(off_rg, c0))).astype(jnp.bfloat16)

    for c0 in range(0, D, MXU_COLS):
        cols = slice(c0, c0 + MXU_COLS)
        yb = _dot(y_scr[...], lwo_ref[:, cols])
        mb_scr[:, cols] = (m_scr[:, cols] + _sigmoid(in_proj(off_gb, c0)) * yb).astype(jnp.bfloat16)

    for c0 in range(0, D, MXU_COLS):
        res = _dot(mb_scr[...], wmix_ref[:, c0:c0 + MXU_COLS])
        for q in range(MXU_COLS // LANES):
            op_scr[c0 // LANES + q] = res[:, q * LANES:(q + 1) * LANES]
    for r in range(0, T, NORM_ROWS):
        for c in range(D // LANES):
            lanes = slice(c * LANES, (c + 1) * LANES)
            o_ref[r:r + NORM_ROWS, lanes] = (x_ref[r:r + NORM_ROWS, lanes]
                                             + op_scr[c, interleaved_rows(r, NORM_ROWS), :])


def _ffn_kernel(x_ref, n2g_ref, w1_ref, w3_ref, w2_ref, nfg_ref, o_ref, h_scr, act_scr, y_scr):
    tm, D = x_ref.shape
    d_ff = w1_ref.shape[1]

    for r in range(0, tm, NORM_ROWS):
        x = x_ref[r:r + NORM_ROWS, :]
        ms = jnp.mean(x * x, axis=-1, keepdims=True)
        h_scr[r:r + NORM_ROWS, :] = (x * lax.rsqrt(ms + EPS) * n2g_ref[...]).astype(jnp.bfloat16)

    for c0 in range(0, d_ff, MXU_COLS):
        cols = slice(c0, c0 + MXU_COLS)
        g = _dot(h_scr[...], w1_ref[:, cols])
        u = _dot(h_scr[...], w3_ref[:, cols])
        act_scr[:, cols] = (_silu(g) * u).astype(jnp.bfloat16)

    for c0 in range(0, D, MXU_COLS):
        cols = slice(c0, c0 + MXU_COLS)
        y_scr[:, cols] = x_ref[:, cols] + _dot(act_scr[...], w2_ref[:, cols])

    for r in range(0, tm, NORM_ROWS):
        y = y_scr[r:r + NORM_ROWS, :]
        ms = jnp.mean(y * y, axis=-1, keepdims=True)
        o_ref[r:r + NORM_ROWS, :] = y * lax.rsqrt(ms + EPS) * nfg_ref[...]


def _resident(shape):
    return pl.BlockSpec(shape, lambda *_: (0,) * len(shape), pipeline_mode=pl.Buffered(1))


def _layer_resident(shape, layer):
    return pl.BlockSpec((None,) + tuple(shape), lambda *_: (layer,) + (0,) * len(shape),
                        pipeline_mode=pl.Buffered(1))


def _stacked_rows(p):
    return p.reshape(p.shape[0], 1, p.shape[1])


def _gate_weights(wa, wx, gate_starts):
    heads, blk, _ = wa.shape
    d_rnn = heads * blk
    head_of = np.arange(d_rnn) // blk
    same_head = jnp.asarray(head_of[:, None] == head_of[None, :])

    def dense(w):
        return jnp.where(same_head, jnp.tile(w.reshape(d_rnn, blk), (1, heads)), 0.0)

    da, dx = dense(wa), dense(wx)
    tiles = []
    for j, ks in enumerate(gate_starts):
        cols = slice(j * MXU_COLS, (j + 1) * MXU_COLS)
        tiles.append(jnp.concatenate([da[ks:ks + GATE_K, cols], dx[ks:ks + GATE_K, cols]], axis=1))
    return jnp.stack(tiles).astype(jnp.bfloat16)


def kernel(x, norm1_g, w_in, b_in, conv_dw_w, conv_dw_b, conv_ln_g, conv_ln_b, conv_w_out, conv_b_out,
           lru_conv_w, lru_conv_b, lru_wa, lru_ba, lru_wx, lru_bx, lru_lambda, lru_w_out, w_mix_out,
           norm2_g, ffn_w1, ffn_w3, ffn_w2, norm_f_g):
    B, S, D = x.shape
    depth = w_in.shape[0]
    d_in = w_in.shape[2]
    conv_k, d_conv = conv_dw_w.shape[1:]
    lru_k, d_rnn = lru_conv_w.shape[1:]
    heads, blk = lru_wa.shape[1:3]
    d_ff = ffn_w1.shape[2]
    T = MIX_TILE
    assert S % T == 0 and (B * S) % FFN_TILE == 0
    assert d_in == 2 * d_conv + 2 * d_rnn + 2 * D and heads * blk == d_rnn
    assert d_conv % MXU_COLS == 0 and d_rnn % MXU_COLS == 0 and D % MXU_COLS == 0 and d_ff % MXU_COLS == 0
    u_halo_rows = SUBLANES * (conv_k - 1)
    rx_halo_rows = SUBLANES * (lru_k - 1)
    gate_starts = _gate_windows(d_rnn, blk)
    n_gate = len(gate_starts)
    bf16 = jnp.bfloat16
    f32 = jnp.float32

    def row(v):
        return v.reshape(1, -1)

    for l in range(depth):
        wg = _gate_weights(lru_wa[l], lru_wx[l], gate_starts)
        mixer = pl.pallas_call(
            functools.partial(_mixer_kernel, gate_starts=gate_starts),
            grid=(B, S // T),
            in_specs=[
                pl.BlockSpec((None, T, D), lambda b, s: (b, s, 0)),
                _layer_resident((1, D), l),
                _resident((D, d_in)), _layer_resident((1, d_in), l),
                _layer_resident((conv_k, d_conv), l), _layer_resident((1, d_conv), l),
                _layer_resident((1, d_conv), l), _layer_resident((1, d_conv), l),
                _resident((d_conv, D)), _layer_resident((1, D), l),
                _layer_resident((lru_k, d_rnn), l), _layer_resident((1, d_rnn), l),
                _resident((n_gate, GATE_K, 2 * MXU_COLS)),
                _layer_resident((1, d_rnn), l), _layer_resident((1, d_rnn), l), _layer_resident((1, d_rnn), l),
                _resident((d_rnn, D)), _resident((D, D)),
            ],
            out_specs=pl.BlockSpec((None, T, D), lambda b, s: (b, s, 0)),
            out_shape=jax.ShapeDtypeStruct((B, S, D), f32),
            scratch_shapes=[
                pltpu.VMEM((D // LANES, T, LANES), f32),
                pltpu.VMEM((T, D), bf16),
                pltpu.VMEM((u_halo_rows + T, d_conv), f32),
                pltpu.VMEM((u_halo_rows, d_conv), f32),
                pltpu.VMEM((T, d_conv), f32),
                pltpu.VMEM((T, d_conv), bf16),
                pltpu.VMEM((rx_halo_rows + T, d_rnn), f32),
                pltpu.VMEM((rx_halo_rows, d_rnn), f32),
                pltpu.VMEM((T, d_rnn), f32),
                pltpu.VMEM((T, d_rnn), bf16),
                pltpu.VMEM((T, d_rnn), bf16),
                pltpu.VMEM((T, D), f32),
                pltpu.VMEM((T, D), bf16),
                pltpu.VMEM((D // LANES, T, LANES), f32),
                pltpu.VMEM((SUBLANES, d_rnn), f32),
            ],
            compiler_params=pltpu.CompilerParams(
                dimension_semantics=("arbitrary", "arbitrary"),
                vmem_limit_bytes=MIX_VMEM_BYTES),
            name="token_mixer",
        )
        x = mixer(
            x, _stacked_rows(norm1_g), w_in[l].astype(bf16), _stacked_rows(b_in),
            conv_dw_w, _stacked_rows(conv_dw_b), _stacked_rows(conv_ln_g), _stacked_rows(conv_ln_b),
            conv_w_out[l].astype(bf16), _stacked_rows(conv_b_out),
            lru_conv_w, _stacked_rows(lru_conv_b), wg, _stacked_rows(lru_ba), _stacked_rows(lru_bx),
            _stacked_rows(lru_lambda),
            lru_w_out[l].astype(bf16), w_mix_out[l].astype(bf16))

        last = l == depth - 1
        ffn = pl.pallas_call(
            _ffn_kernel,
            grid=(B * S // FFN_TILE,),
            in_specs=[
                pl.BlockSpec((FFN_TILE, D), lambda i: (i, 0)),
                _layer_resident((1, D), l),
                _resident((D, d_ff)), _resident((D, d_ff)), _resident((d_ff, D)),
                _resident((1, D)),
            ],
            out_specs=pl.BlockSpec((FFN_TILE, D), lambda i: (i, 0)),
            out_shape=jax.ShapeDtypeStruct((B * S, D), f32),
            scratch_shapes=[
                pltpu.VMEM((FFN_TILE, D), bf16),
                pltpu.VMEM((FFN_TILE, d_ff), bf16),
                pltpu.VMEM((FFN_TILE, D), f32),
            ],
            compiler_params=pltpu.CompilerParams(
                dimension_semantics=("arbitrary",),
                vmem_limit_bytes=FFN_VMEM_BYTES),
            name="channel_mixer",
        )
        assert last, "the final rmsnorm is fused into the last layer's channel mixer"
        x = ffn(x.reshape(B * S, D), _stacked_rows(norm2_g), ffn_w1[l].astype(bf16), ffn_w3[l].astype(bf16),
                ffn_w2[l].astype(bf16), row(norm_f_g)).reshape(B, S, D)
    return x
```

```python
import functools
import math

import jax
import jax.numpy as jnp
import numpy as np
from jax import lax
from jax.experimental import pallas as pl
from jax.experimental.pallas import tpu as pltpu

EPS = 1e-6
LRU_C = 8.0

SUBLANES = 8
LANES = 128
MXU_COLS = 256

MIX_TILE = 512
FFN_TILE = 1024
CONV_ROWS = 64
CONV_LAG = 1
NORM_ROWS = 32
GATE_K = 512

MIX_VMEM_BYTES = 60 * 1024 * 1024
FFN_VMEM_BYTES = 56 * 1024 * 1024


def _sigmoid(x):
    return 0.5 * jnp.tanh(0.5 * x) + 0.5


def _silu(x):
    return x * _sigmoid(x)


def _gelu_tanh(x):
    c = math.sqrt(2.0 / math.pi)
    return 0.5 * x * (1.0 + jnp.tanh(c * (x + 0.044715 * (x * x * x))))


def _dot(a, b):
    return jnp.dot(a, b, preferred_element_type=jnp.float32)


def _gate_windows(d_rnn, block):
    starts = []
    for j in range(d_rnn // MXU_COLS):
        lo = (j * MXU_COLS // block) * block
        hi = -(-((j + 1) * MXU_COLS) // block) * block
        start = min(lo // LANES * LANES, d_rnn - GATE_K)
        assert start <= lo and hi <= start + GATE_K
        starts.append(start)
    return tuple(starts)


def _mixer_kernel(x_ref, n1g_ref, win_ref, bin_ref, cw_ref, cb_ref, lng_ref, lnb_ref, cwo_ref, cbo_ref,
                  lcw_ref, lcb_ref, wg_ref, ba_ref, bx_ref, lam_ref, lwo_ref, wmix_ref,
                  o_ref,
                  hp_scr, h_scr, u_scr, uprev_scr, c_scr, v_scr, rx_scr, rxprev_scr, xb_scr, xbb_scr,
                  y_scr, m_scr, mb_scr, op_scr, hcar_scr,
                  *, gate_starts):
    T, D = x_ref.shape
    L = T // SUBLANES
    d_conv = cwo_ref.shape[0]
    d_rnn = lwo_ref.shape[0]
    u_halo = cw_ref.shape[0] - 1
    rx_halo = lcw_ref.shape[0] - 1
    assert u_halo <= L and rx_halo <= L and L % NORM_ROWS == 0 and T % CONV_ROWS == 0
    off_glu_b = d_conv
    off_rx = 2 * d_conv
    off_rg = off_rx + d_rnn
    off_ga = off_rg + d_rnn
    off_gb = off_ga + D

    @pl.when(pl.program_id(1) == 0)
    def _():
        uprev_scr[...] = jnp.zeros(uprev_scr.shape, jnp.float32)
        rxprev_scr[...] = jnp.zeros(rxprev_scr.shape, jnp.float32)
        hcar_scr[...] = jnp.zeros(hcar_scr.shape, jnp.float32)

    def interleaved_rows(r, n):
        s, i0 = divmod(r, L)
        return pl.ds(SUBLANES * i0 + s, n, stride=SUBLANES)

    for r in range(0, T, NORM_ROWS):
        x = x_ref[r:r + NORM_ROWS, :]
        ms = jnp.mean(x * x, axis=-1, keepdims=True)
        h = x * lax.rsqrt(ms + EPS) * n1g_ref[...]
        for c in range(D // LANES):
            hp_scr[c, interleaved_rows(r, NORM_ROWS), :] = h[:, c * LANES:(c + 1) * LANES]
    for c in range(D // LANES):
        h_scr[:, c * LANES:(c + 1) * LANES] = hp_scr[c].astype(jnp.bfloat16)

    def in_proj(off, c0, width=MXU_COLS):
        cols = slice(off + c0, off + c0 + width)
        return _dot(h_scr[...], win_ref[:, cols]) + bin_ref[:, cols]

    def fill_halo(buf, prev, halo):
        n, width = SUBLANES * halo, buf.shape[1]
        tail = buf[T:T + n, :].reshape(halo, SUBLANES, width)
        sub = lax.broadcasted_iota(jnp.int32, tail.shape, 1)
        mixed = jnp.where(sub == SUBLANES - 1, prev[...].reshape(tail.shape), tail)
        buf[0:n, :] = pltpu.roll(mixed, 1, 1).reshape(n, width)
        prev[...] = tail.reshape(n, width)

    always = pl.program_id(1) >= 0

    def causal_conv(buf, w_ref, b_ref, emit):
        groups = CONV_ROWS // SUBLANES
        prev = None
        for l0 in range(0, buf.shape[1], LANES):
            lanes = slice(l0, l0 + LANES)
            for r in range(0, T, CONV_ROWS):
                acc = jnp.broadcast_to(b_ref[:, lanes], (groups, SUBLANES, LANES))
                if prev is not None:
                    acc = jnp.where(always, acc, prev)
                sums = [acc[0]]
                for k in range(w_ref.shape[0]):
                    row = r + SUBLANES * k
                    wk = jnp.broadcast_to(w_ref[k:k + 1, lanes], (SUBLANES, LANES))
                    if k >= CONV_LAG:
                        wk = jnp.where(always, wk, sums[k - CONV_LAG])
                    acc = acc + wk[None] * buf[row:row + CONV_ROWS, lanes].reshape(groups, SUBLANES, LANES)
                    sums.append(acc[0])
                emit(r, lanes, acc.reshape(CONV_ROWS, LANES))
                prev = acc

    u0 = SUBLANES * u_halo
    for c0 in range(0, d_conv, MXU_COLS):
        za = in_proj(0, c0)
        zb = in_proj(off_glu_b, c0)
        u_scr[u0:u0 + T, c0:c0 + MXU_COLS] = za * _sigmoid(zb)
    fill_halo(u_scr, uprev_scr, u_halo)

    def emit_conv(r, lanes, acc):
        c_scr[r:r + CONV_ROWS, lanes] = acc
    causal_conv(u_scr, cw_ref, cb_ref, emit_conv)

    for r in range(0, T, NORM_ROWS):
        cv = c_scr[r:r + NORM_ROWS, :]
        mu = jnp.mean(cv, axis=-1, keepdims=True)
        xc = cv - mu
        var = jnp.mean(xc * xc, axis=-1, keepdims=True)
        y = xc * lax.rsqrt(var + EPS) * lng_ref[...] + lnb_ref[...]
        v_scr[r:r + NORM_ROWS, :] = _silu(y).astype(jnp.bfloat16)

    for c0 in range(0, D, MXU_COLS):
        cols = slice(c0, c0 + MXU_COLS)
        ya = _dot(v_scr[...], cwo_ref[:, cols]) + cbo_ref[:, cols]
        m_scr[:, cols] = _sigmoid(in_proj(off_ga, c0)) * ya

    rx0 = SUBLANES * rx_halo
    for c0 in range(0, d_rnn, MXU_COLS):
        rx_scr[rx0:rx0 + T, c0:c0 + MXU_COLS] = in_proj(off_rx, c0)
    fill_halo(rx_scr, rxprev_scr, rx_halo)

    def emit_rx(r, lanes, acc):
        xb_scr[r:r + CONV_ROWS, lanes] = acc
        xbb_scr[r:r + CONV_ROWS, lanes] = acc.astype(jnp.bfloat16)
    causal_conv(rx_scr, lcw_ref, lcb_ref, emit_rx)

    lam = lam_ref[...]
    neg_c_sp = -LRU_C * (jnp.maximum(-lam, 0.0) + jnp.log1p(jnp.exp(-jnp.abs(lam))))

    sub = lax.broadcasted_iota(jnp.int32, (SUBLANES, MXU_COLS), 0)
    for j, ks in enumerate(gate_starts):
        c0 = j * MXU_COLS
        cols = slice(c0, c0 + MXU_COLS)
        g = _dot(xbb_scr[:, ks:ks + GATE_K], wg_ref[j])
        r_gate = _sigmoid(g[:, :MXU_COLS] + ba_ref[:, cols])
        i_gate = _sigmoid(g[:, MXU_COLS:] + bx_ref[:, cols])
        log_a = neg_c_sp[:, cols] * r_gate
        a = jnp.exp(log_a)
        mult = jnp.sqrt(-jnp.tanh(log_a) * (1.0 + a * a))
        b = mult * (i_gate * xb_scr[:, cols])

        a3 = a.reshape(L, SUBLANES, MXU_COLS)
        b3 = b.reshape(L, SUBLANES, MXU_COLS)
        hloc, prod = [b3[0]], [a3[0]]
        for i in range(1, L):
            hloc.append(a3[i] * hloc[-1] + b3[i])
            prod.append(a3[i] * prod[-1])
        ea, eb = prod[-1], hloc[-1]
        d = 1
        while d < SUBLANES:
            keep = sub >= d
            ea_sh = jnp.where(keep, pltpu.roll(ea, d, 0), 1.0)
            eb_sh = jnp.where(keep, pltpu.roll(eb, d, 0), 0.0)
            eb = eb + ea * eb_sh
            ea = ea * ea_sh
            d *= 2
        h_in = hcar_scr[:, cols]
        ends = eb + ea * h_in
        start = jnp.where(sub == 0, h_in, pltpu.roll(ends, 1, 0))
        hcar_scr[:, cols] = jnp.broadcast_to(ends[SUBLANES - 1:SUBLANES, :], (SUBLANES, MXU_COLS))
        hseq = jnp.concatenate([hl + pr * start for hl, pr in zip(hloc, prod)], axis=0)

        y_scr[:, cols] = (hseq * _gelu_tanh(in_proj(off_rg, c0))).astype(jnp.bfloat16)

    for c0 in range(0, D, MXU_COLS):
        cols = slice(c0, c0 + MXU_COLS)
        yb = _dot(y_scr[...], lwo_ref[:, cols])
        mb_scr[:, cols] = (m_scr[:, cols] + _sigmoid(in_proj(off_gb, c0)) * yb).astype(jnp.bfloat16)

    for c0 in range(0, D, MXU_COLS):
        res = _dot(mb_scr[...], wmix_ref[:, c0:c0 + MXU_COLS])
        for q in range(MXU_COLS // LANES):
            op_scr[c0 // LANES + q] = res[:, q * LANES:(q + 1) * LANES]
    for r in range(0, T, NORM_ROWS):
        for c in range(D // LANES):
            lanes = slice(c * LANES, (c + 1) * LANES)
            o_ref[r:r + NORM_ROWS, lanes] = (x_ref[r:r + NORM_ROWS, lanes]
                                             + op_scr[c, interleaved_rows(r, NORM_ROWS), :])


def _ffn_kernel(x_ref, n2g_ref, w1_ref, w3_ref, w2_ref, nfg_ref, o_ref, h_scr, act_scr, y_scr):
    tm, D = x_ref.shape
    d_ff = w1_ref.shape[1]

    for r in range(0, tm, NORM_ROWS):
        x = x_ref[r:r + NORM_ROWS, :]
        ms = jnp.mean(x * x, axis=-1, keepdims=True)
        h_scr[r:r + NORM_ROWS, :] = (x * lax.rsqrt(ms + EPS) * n2g_ref[...]).astype(jnp.bfloat16)

    for c0 in range(0, d_ff, MXU_COLS):
        cols = slice(c0, c0 + MXU_COLS)
        g = _dot(h_scr[...], w1_ref[:, cols])
        u = _dot(h_scr[...], w3_ref[:, cols])
        act_scr[:, cols] = (_silu(g) * u).astype(jnp.bfloat16)

    for c0 in range(0, D, MXU_COLS):
        cols = slice(c0, c0 + MXU_COLS)
        y_scr[:, cols] = x_ref[:, cols] + _dot(act_scr[...], w2_ref[:, cols])

    for r in range(0, tm, NORM_ROWS):
        y = y_scr[r:r + NORM_ROWS, :]
        ms = jnp.mean(y * y, axis=-1, keepdims=True)
        o_ref[r:r + NORM_ROWS, :] = y * lax.rsqrt(ms + EPS) * nfg_ref[...]


def _resident(shape):
    return pl.BlockSpec(shape, lambda *_: (0,) * len(shape), pipeline_mode=pl.Buffered(1))


def _layer_resident(shape, layer):
    return pl.BlockSpec((None,) + tuple(shape), lambda *_: (layer,) + (0,) * len(shape),
                        pipeline_mode=pl.Buffered(1))


def _stacked_rows(p):
    return p.reshape(p.shape[0], 1, p.shape[1])


def _gate_weights(wa, wx, gate_starts):
    heads, blk, _ = wa.shape
    d_rnn = heads * blk
    head_of = np.arange(d_rnn) // blk
    same_head = jnp.asarray(head_of[:, None] == head_of[None, :])

    def dense(w):
        return jnp.where(same_head, jnp.tile(w.reshape(d_rnn, blk), (1, heads)), 0.0)

    da, dx = dense(wa), dense(wx)
    tiles = []
    for j, ks in enumerate(gate_starts):
        cols = slice(j * MXU_COLS, (j + 1) * MXU_COLS)
        tiles.append(jnp.concatenate([da[ks:ks + GATE_K, cols], dx[ks:ks + GATE_K, cols]], axis=1))
    return jnp.stack(tiles).astype(jnp.bfloat16)


def kernel(x, norm1_g, w_in, b_in, conv_dw_w, conv_dw_b, conv_ln_g, conv_ln_b, conv_w_out, conv_b_out,
           lru_conv_w, lru_conv_b, lru_wa, lru_ba, lru_wx, lru_bx, lru_lambda, lru_w_out, w_mix_out,
           norm2_g, ffn_w1, ffn_w3, ffn_w2, norm_f_g):
    B, S, D = x.shape
    depth = w_in.shape[0]
    d_in = w_in.shape[2]
    conv_k, d_conv = conv_dw_w.shape[1:]
    lru_k, d_rnn = lru_conv_w.shape[1:]
    heads, blk = lru_wa.shape[1:3]
    d_ff = ffn_w1.shape[2]
    T = MIX_TILE
    assert S % T == 0 and (B * S) % FFN_TILE == 0
    assert d_in == 2 * d_conv + 2 * d_rnn + 2 * D and heads * blk == d_rnn
    assert d_conv % MXU_COLS == 0 and d_rnn % MXU_COLS == 0 and D % MXU_COLS == 0 and d_ff % MXU_COLS == 0
    u_halo_rows = SUBLANES * (conv_k - 1)
    rx_halo_rows = SUBLANES * (lru_k - 1)
    gate_starts = _gate_windows(d_rnn, blk)
    n_gate = len(gate_starts)
    bf16 = jnp.bfloat16
    f32 = jnp.float32

    def row(v):
        return v.reshape(1, -1)

    for l in range(depth):
        wg = _gate_weights(lru_wa[l], lru_wx[l], gate_starts)
        mixer = pl.pallas_call(
            functools.partial(_mixer_kernel, gate_starts=gate_starts),
            grid=(B, S // T),
            in_specs=[
                pl.BlockSpec((None, T, D), lambda b, s: (b, s, 0)),
                _layer_resident((1, D), l),
                _resident((D, d_in)), _layer_resident((1, d_in), l),
                _layer_resident((conv_k, d_conv), l), _layer_resident((1, d_conv), l),
                _layer_resident((1, d_conv), l), _layer_resident((1, d_conv), l),
                _resident((d_conv, D)), _layer_resident((1, D), l),
                _layer_resident((lru_k, d_rnn), l), _layer_resident((1, d_rnn), l),
                _resident((n_gate, GATE_K, 2 * MXU_COLS)),
                _layer_resident((1, d_rnn), l), _layer_resident((1, d_rnn), l), _layer_resident((1, d_rnn), l),
                _resident((d_rnn, D)), _resident((D, D)),
            ],
            out_specs=pl.BlockSpec((None, T, D), lambda b, s: (b, s, 0)),
            out_shape=jax.ShapeDtypeStruct((B, S, D), f32),
            scratch_shapes=[
                pltpu.VMEM((D // LANES, T, LANES), f32),
                pltpu.VMEM((T, D), bf16),
                pltpu.VMEM((u_halo_rows + T, d_conv), f32),
                pltpu.VMEM((u_halo_rows, d_conv), f32),
                pltpu.VMEM((T, d_conv), f32),
                pltpu.VMEM((T, d_conv), bf16),
                pltpu.VMEM((rx_halo_rows + T, d_rnn), f32),
                pltpu.VMEM((rx_halo_rows, d_rnn), f32),
                pltpu.VMEM((T, d_rnn), f32),
                pltpu.VMEM((T, d_rnn), bf16),
                pltpu.VMEM((T, d_rnn), bf16),
                pltpu.VMEM((T, D), f32),
                pltpu.VMEM((T, D), bf16),
                pltpu.VMEM((D // LANES, T, LANES), f32),
                pltpu.VMEM((SUBLANES, d_rnn), f32),
            ],
            compiler_params=pltpu.CompilerParams(
                dimension_semantics=("arbitrary", "arbitrary"),
                vmem_limit_bytes=MIX_VMEM_BYTES),
            name="token_mixer",
        )
        x = mixer(
            x, _stacked_rows(norm1_g), w_in[l].astype(bf16), _stacked_rows(b_in),
            conv_dw_w, _stacked_rows(conv_dw_b), _stacked_rows(conv_ln_g), _stacked_rows(conv_ln_b),
            conv_w_out[l].astype(bf16), _stacked_rows(conv_b_out),
            lru_conv_w, _stacked_rows(lru_conv_b), wg, _stacked_rows(lru_ba), _stacked_rows(lru_bx),
            _stacked_rows(lru_lambda),
            lru_w_out[l].astype(bf16), w_mix_out[l].astype(bf16))

        last = l == depth - 1
        ffn = pl.pallas_call(
            _ffn_kernel,
            grid=(B * S // FFN_TILE,),
            in_specs=[
                pl.BlockSpec((FFN_TILE, D), lambda i: (i, 0)),
                _layer_resident((1, D), l),
                _resident((D, d_ff)), _resident((D, d_ff)), _resident((d_ff, D)),
                _resident((1, D)),
            ],
            out_specs=pl.BlockSpec((FFN_TILE, D), lambda i: (i, 0)),
            out_shape=jax.ShapeDtypeStruct((B * S, D), f32),
            scratch_shapes=[
                pltpu.VMEM((FFN_TILE, D), bf16),
                pltpu.VMEM((FFN_TILE, d_ff), bf16),
                pltpu.VMEM((FFN_TILE, D), f32),
            ],
            compiler_params=pltpu.CompilerParams(
                dimension_semantics=("arbitrary",),
                vmem_limit_bytes=FFN_VMEM_BYTES),
            name="channel_mixer",
        )
        assert last, "the final rmsnorm is fused into the last layer's channel mixer"
        x = ffn(x.reshape(B * S, D), _stacked_rows(norm2_g), ffn_w1[l].astype(bf16), ffn_w3[l].astype(bf16),
                ffn_w2[l].astype(bf16), row(norm_f_g)).reshape(B, S, D)
    return x
```

```python
import functools
import math

import jax
import jax.numpy as jnp
import numpy as np
from jax import lax
from jax.experimental import pallas as pl
from jax.experimental.pallas import tpu as pltpu

EPS = 1e-6
LRU_C = 8.0

SUBLANES = 8
LANES = 128
MXU_COLS = 256

MIX_TILE = 512
FFN_TILE = 1024
CONV_ROWS = 128
CONV_LAG = 1
NORM_ROWS = 32
GATE_K = 512

MIX_VMEM_BYTES = 60 * 1024 * 1024
FFN_VMEM_BYTES = 56 * 1024 * 1024


def _sigmoid(x):
    return 0.5 * jnp.tanh(0.5 * x) + 0.5


def _silu(x):
    return x * _sigmoid(x)


def _gelu_tanh(x):
    c = math.sqrt(2.0 / math.pi)
    return 0.5 * x * (1.0 + jnp.tanh(c * (x + 0.044715 * (x * x * x))))


def _dot(a, b):
    return jnp.dot(a, b, preferred_element_type=jnp.float32)


def _gate_windows(d_rnn, block):
    starts = []
    for j in range(d_rnn // MXU_COLS):
        lo = (j * MXU_COLS // block) * block
        hi = -(-((j + 1) * MXU_COLS) // block) * block
        start = min(lo // LANES * LANES, d_rnn - GATE_K)
        assert start <= lo and hi <= start + GATE_K
        starts.append(start)
    return tuple(starts)


def _mixer_kernel(x_ref, n1g_ref, win_ref, bin_ref, cw_ref, cb_ref, lng_ref, lnb_ref, cwo_ref, cbo_ref,
                  lcw_ref, lcb_ref, wg_ref, ba_ref, bx_ref, lam_ref, lwo_ref, wmix_ref,
                  o_ref,
                  hp_scr, h_scr, u_scr, uprev_scr, c_scr, v_scr, rx_scr, rxprev_scr, xb_scr, xbb_scr,
                  y_scr, m_scr, mb_scr, op_scr, hcar_scr,
                  *, gate_starts):
    T, D = x_ref.shape
    L = T // SUBLANES
    d_conv = cwo_ref.shape[0]
    d_rnn = lwo_ref.shape[0]
    u_halo = cw_ref.shape[0] - 1
    rx_halo = lcw_ref.shape[0] - 1
    assert u_halo <= L and rx_halo <= L and L % NORM_ROWS == 0 and T % CONV_ROWS == 0
    off_glu_b = d_conv
    off_rx = 2 * d_conv
    off_rg = off_rx + d_rnn
    off_ga = off_rg + d_rnn
    off_gb = off_ga + D

    @pl.when(pl.program_id(1) == 0)
    def _():
        uprev_scr[...] = jnp.zeros(uprev_scr.shape, jnp.float32)
        rxprev_scr[...] = jnp.zeros(rxprev_scr.shape, jnp.float32)
        hcar_scr[...] = jnp.zeros(hcar_scr.shape, jnp.float32)

    def interleaved_rows(r, n):
        s, i0 = divmod(r, L)
        return pl.ds(SUBLANES * i0 + s, n, stride=SUBLANES)

    for r in range(0, T, NORM_ROWS):
        x = x_ref[r:r + NORM_ROWS, :]
        ms = jnp.mean(x * x, axis=-1, keepdims=True)
        h = x * lax.rsqrt(ms + EPS) * n1g_ref[...]
        for c in range(D // LANES):
            hp_scr[c, interleaved_rows(r, NORM_ROWS), :] = h[:, c * LANES:(c + 1) * LANES]
    for c in range(D // LANES):
        h_scr[:, c * LANES:(c + 1) * LANES] = hp_scr[c].astype(jnp.bfloat16)

    def in_proj(off, c0, width=MXU_COLS):
        cols = slice(off + c0, off + c0 + width)
        return _dot(h_scr[...], win_ref[:, cols]) + bin_ref[:, cols]

    def fill_halo(buf, prev, halo):
        n, width = SUBLANES * halo, buf.shape[1]
        tail = buf[T:T + n, :].reshape(halo, SUBLANES, width)
        sub = lax.broadcasted_iota(jnp.int32, tail.shape, 1)
        mixed = jnp.where(sub == SUBLANES - 1, prev[...].reshape(tail.shape), tail)
        buf[0:n, :] = pltpu.roll(mixed, 1, 1).reshape(n, width)
        prev[...] = tail.reshape(n, width)

    always = pl.program_id(1) >= 0
    zero_rows = pl.multiple_of(jnp.minimum(pl.program_id(1), 0) * SUBLANES, SUBLANES)

    def causal_conv(buf, w_ref, b_ref, emit):
        groups = CONV_ROWS // SUBLANES
        prev = None
        for l0 in range(0, buf.shape[1], LANES):
            lanes = slice(l0, l0 + LANES)
            for r in range(0, T, CONV_ROWS):
                acc = jnp.broadcast_to(b_ref[:, lanes], (groups, SUBLANES, LANES))
                if prev is not None:
                    acc = jnp.where(always, acc, prev)
                sums = [acc[0]]
                for k in range(w_ref.shape[0]):
                    row = r + SUBLANES * k
                    wk = jnp.broadcast_to(w_ref[k:k + 1, lanes], (SUBLANES, LANES))
                    if k >= CONV_LAG:
                        wk = jnp.where(always, wk, sums[k - CONV_LAG])
                    acc = acc + wk[None] * buf[row:row + CONV_ROWS, lanes].reshape(groups, SUBLANES, LANES)
                    sums.append(acc[0])
                emit(r, lanes, acc.reshape(CONV_ROWS, LANES))
                prev = acc

    u0 = SUBLANES * u_halo
    for c0 in range(0, d_conv, MXU_COLS):
        za = in_proj(0, c0)
        zb = in_proj(off_glu_b, c0)
        u_scr[pl.ds(zero_rows + u0, T), c0:c0 + MXU_COLS] = za * _sigmoid(zb)
    fill_halo(u_scr, uprev_scr, u_halo)

    def emit_conv(r, lanes, acc):
        c_scr[pl.ds(zero_rows + r, CONV_ROWS), lanes] = acc
    causal_conv(u_scr, cw_ref, cb_ref, emit_conv)

    for r in range(0, T, NORM_ROWS):
        cv = c_scr[r:r + NORM_ROWS, :]
        mu = jnp.mean(cv, axis=-1, keepdims=True)
        xc = cv - mu
        var = jnp.mean(xc * xc, axis=-1, keepdims=True)
        y = xc * lax.rsqrt(var + EPS) * lng_ref[...] + lnb_ref[...]
        v_scr[r:r + NORM_ROWS, :] = _silu(y).astype(jnp.bfloat16)

    for c0 in range(0, D, MXU_COLS):
        cols = slice(c0, c0 + MXU_COLS)
        ya = _dot(v_scr[...], cwo_ref[:, cols]) + cbo_ref[:, cols]
        m_scr[:, cols] = _sigmoid(in_proj(off_ga, c0)) * ya

    rx0 = SUBLANES * rx_halo
    for c0 in range(0, d_rnn, MXU_COLS):
        rx_scr[rx0:rx0 + T, c0:c0 + MXU_COLS] = in_proj(off_rx, c0)
    fill_halo(rx_scr, rxprev_scr, rx_halo)

    def emit_rx(r, lanes, acc):
        xb_scr[r:r + CONV_ROWS, lanes] = acc
        xbb_scr[r:r + CONV_ROWS, lanes] = acc.astype(jnp.bfloat16)
    causal_conv(rx_scr, lcw_ref, lcb_ref, emit_rx)

    lam = lam_ref[...]
    neg_c_sp = -LRU_C * (jnp.maximum(-lam, 0.0) + jnp.log1p(jnp.exp(-jnp.abs(lam))))

    sub = lax.broadcasted_iota(jnp.int32, (SUBLANES, MXU_COLS), 0)
    for j, ks in enumerate(gate_starts):
        c0 = j * MXU_COLS
        cols = slice(c0, c0 + MXU_COLS)
        g = _dot(xbb_scr[:, ks:ks + GATE_K], wg_ref[j])
        r_gate = _sigmoid(g[:, :MXU_COLS] + ba_ref[:, cols])
        i_gate = _sigmoid(g[:, MXU_COLS:] + bx_ref[:, cols])
        log_a = neg_c_sp[:, cols] * r_gate
        a = jnp.exp(log_a)
        mult = jnp.sqrt(-jnp.tanh(log_a) * (1.0 + a * a))
        b = mult * (i_gate * xb_scr[:, cols])

        a3 = a.reshape(L, SUBLANES, MXU_COLS)
        b3 = b.reshape(L, SUBLANES, MXU_COLS)
        hloc, prod = [b3[0]], [a3[0]]
        for i in range(1, L):
            hloc.append(a3[i] * hloc[-1] + b3[i])
            prod.append(a3[i] * prod[-1])
        ea, eb = prod[-1], hloc[-1]
        d = 1
        while d < SUBLANES:
            keep = sub >= d
            ea_sh = jnp.where(keep, pltpu.roll(ea, d, 0), 1.0)
            eb_sh = jnp.where(keep, pltpu.roll(eb, d, 0), 0.0)
            eb = eb + ea * eb_sh
            ea = ea * ea_sh
            d *= 2
        h_in = hcar_scr[:, cols]
        ends = eb + ea * h_in
        start = jnp.where(sub == 0, h_in, pltpu.roll(ends, 1, 0))
        hcar_scr[:, cols] = jnp.broadcast_to(ends[SUBLANES - 1:SUBLANES, :], (SUBLANES, MXU_COLS))
        hseq = jnp.concatenate([hl + pr * start for hl, pr in zip(hloc, prod)], axis=0)

        y_scr[:, cols] = (hseq * _gelu_tanh(in_proj(off_rg, c0))).astype(jnp.bfloat16)

    for c0 in range(0, D, MXU_COLS):
        cols = slice(c0, c0 + MXU_COLS)
        yb = _dot(y_scr[...], lwo_ref[:, cols])
        mb_scr[:, cols] = (m_scr[:, cols] + _sigmoid(in_proj(off_gb, c0)) * yb).astype(jnp.bfloat16)

    for c0 in range(0, D, MXU_COLS):
        res = _dot(mb_scr[...], wmix_ref[:, c0:c0 + MXU_COLS])
        for q in range(MXU_COLS // LANES):
            op_scr[c0 // LANES + q] = res[:, q * LANES:(q + 1) * LANES]
    for r in range(0, T, NORM_ROWS):
        for c in range(D // LANES):
            lanes = slice(c * LANES, (c + 1) * LANES)
            o_ref[r:r + NORM_ROWS, lanes] = (x_ref[r:r + NORM_ROWS, lanes]
                                             + op_scr[c, interleaved_rows(r, NORM_ROWS), :])


def _ffn_kernel(x_ref, n2g_ref, w1_ref, w3_ref, w2_ref, nfg_ref, o_ref, h_scr, act_scr, y_scr):
    tm, D = x_ref.shape
    d_ff = w1_ref.shape[1]

    for r in range(0, tm, NORM_ROWS):
        x = x_ref[r:r + NORM_ROWS, :]
        ms = jnp.mean(x * x, axis=-1, keepdims=True)
        h_scr[r:r + NORM_ROWS, :] = (x * lax.rsqrt(ms + EPS) * n2g_ref[...]).astype(jnp.bfloat16)

    for c0 in range(0, d_ff, MXU_COLS):
        cols = slice(c0, c0 + MXU_COLS)
        g = _dot(h_scr[...], w1_ref[:, cols])
        u = _dot(h_scr[...], w3_ref[:, cols])
        act_scr[:, cols] = (_silu(g) * u).astype(jnp.bfloat16)

    for c0 in range(0, D, MXU_COLS):
        cols = slice(c0, c0 + MXU_COLS)
        y_scr[:, cols] = x_ref[:, cols] + _dot(act_scr[...], w2_ref[:, cols])

    for r in range(0, tm, NORM_ROWS):
        y = y_scr[r:r + NORM_ROWS, :]
        ms = jnp.mean(y * y, axis=-1, keepdims=True)
        o_ref[r:r + NORM_ROWS, :] = y * lax.rsqrt(ms + EPS) * nfg_ref[...]


def _resident(shape):
    return pl.BlockSpec(shape, lambda *_: (0,) * len(shape), pipeline_mode=pl.Buffered(1))


def _layer_resident(shape, layer):
    return pl.BlockSpec((None,) + tuple(shape), lambda *_: (layer,) + (0,) * len(shape),
                        pipeline_mode=pl.Buffered(1))


def _stacked_rows(p):
    return p.reshape(p.shape[0], 1, p.shape[1])


def _gate_weights(wa, wx, gate_starts):
    heads, blk, _ = wa.shape
    d_rnn = heads * blk
    head_of = np.arange(d_rnn) // blk
    same_head = jnp.asarray(head_of[:, None] == head_of[None, :])

    def dense(w):
        return jnp.where(same_head, jnp.tile(w.reshape(d_rnn, blk), (1, heads)), 0.0)

    da, dx = dense(wa), dense(wx)
    tiles = []
    for j, ks in enumerate(gate_starts):
        cols = slice(j * MXU_COLS, (j + 1) * MXU_COLS)
        tiles.append(jnp.concatenate([da[ks:ks + GATE_K, cols], dx[ks:ks + GATE_K, cols]], axis=1))
    return jnp.stack(tiles).astype(jnp.bfloat16)


def kernel(x, norm1_g, w_in, b_in, conv_dw_w, conv_dw_b, conv_ln_g, conv_ln_b, conv_w_out, conv_b_out,
           lru_conv_w, lru_conv_b, lru_wa, lru_ba, lru_wx, lru_bx, lru_lambda, lru_w_out, w_mix_out,
           norm2_g, ffn_w1, ffn_w3, ffn_w2, norm_f_g):
    B, S, D = x.shape
    depth = w_in.shape[0]
    d_in = w_in.shape[2]
    conv_k, d_conv = conv_dw_w.shape[1:]
    lru_k, d_rnn = lru_conv_w.shape[1:]
    heads, blk = lru_wa.shape[1:3]
    d_ff = ffn_w1.shape[2]
    T = MIX_TILE
    assert S % T == 0 and (B * S) % FFN_TILE == 0
    assert d_in == 2 * d_conv + 2 * d_rnn + 2 * D and heads * blk == d_rnn
    assert d_conv % MXU_COLS == 0 and d_rnn % MXU_COLS == 0 and D % MXU_COLS == 0 and d_ff % MXU_COLS == 0
    u_halo_rows = SUBLANES * (conv_k - 1)
    rx_halo_rows = SUBLANES * (lru_k - 1)
    gate_starts = _gate_windows(d_rnn, blk)
    n_gate = len(gate_starts)
    bf16 = jnp.bfloat16
    f32 = jnp.float32

    def row(v):
        return v.reshape(1, -1)

    for l in range(depth):
        wg = _gate_weights(lru_wa[l], lru_wx[l], gate_starts)
        mixer = pl.pallas_call(
            functools.partial(_mixer_kernel, gate_starts=gate_starts),
            grid=(B, S // T),
            in_specs=[
                pl.BlockSpec((None, T, D), lambda b, s: (b, s, 0)),
                _layer_resident((1, D), l),
                _resident((D, d_in)), _layer_resident((1, d_in), l),
                _layer_resident((conv_k, d_conv), l), _layer_resident((1, d_conv), l),
                _layer_resident((1, d_conv), l), _layer_resident((1, d_conv), l),
                _resident((d_conv, D)), _layer_resident((1, D), l),
                _layer_resident((lru_k, d_rnn), l), _layer_resident((1, d_rnn), l),
                _resident((n_gate, GATE_K, 2 * MXU_COLS)),
                _layer_resident((1, d_rnn), l), _layer_resident((1, d_rnn), l), _layer_resident((1, d_rnn), l),
                _resident((d_rnn, D)), _resident((D, D)),
            ],
            out_specs=pl.BlockSpec((None, T, D), lambda b, s: (b, s, 0)),
            out_shape=jax.ShapeDtypeStruct((B, S, D), f32),
            scratch_shapes=[
                pltpu.VMEM((D // LANES, T, LANES), f32),
                pltpu.VMEM((T, D), bf16),
                pltpu.VMEM((u_halo_rows + T, d_conv), f32),
                pltpu.VMEM((u_halo_rows, d_conv), f32),
                pltpu.VMEM((T, d_conv), f32),
                pltpu.VMEM((T, d_conv), bf16),
                pltpu.VMEM((rx_halo_rows + T, d_rnn), f32),
                pltpu.VMEM((rx_halo_rows, d_rnn), f32),
                pltpu.VMEM((T, d_rnn), f32),
                pltpu.VMEM((T, d_rnn), bf16),
                pltpu.VMEM((T, d_rnn), bf16),
                pltpu.VMEM((T, D), f32),
                pltpu.VMEM((T, D), bf16),
                pltpu.VMEM((D // LANES, T, LANES), f32),
                pltpu.VMEM((SUBLANES, d_rnn), f32),
            ],
            compiler_params=pltpu.CompilerParams(
                dimension_semantics=("arbitrary", "arbitrary"),
                vmem_limit_bytes=MIX_VMEM_BYTES),
            name="token_mixer",
        )
        x = mixer(
            x, _stacked_rows(norm1_g), w_in[l].astype(bf16), _stacked_rows(b_in),
            conv_dw_w, _stacked_rows(conv_dw_b), _stacked_rows(conv_ln_g), _stacked_rows(conv_ln_b),
            conv_w_out[l].astype(bf16), _stacked_rows(conv_b_out),
            lru_conv_w, _stacked_rows(lru_conv_b), wg, _stacked_rows(lru_ba), _stacked_rows(lru_bx),
            _stacked_rows(lru_lambda),
            lru_w_out[l].astype(bf16), w_mix_out[l].astype(bf16))

        last = l == depth - 1
        ffn = pl.pallas_call(
            _ffn_kernel,
            grid=(B * S // FFN_TILE,),
            in_specs=[
                pl.BlockSpec((FFN_TILE, D), lambda i: (i, 0)),
                _layer_resident((1, D), l),
                _resident((D, d_ff)), _resident((D, d_ff)), _resident((d_ff, D)),
                _resident((1, D)),
            ],
            out_specs=pl.BlockSpec((FFN_TILE, D), lambda i: (i, 0)),
            out_shape=jax.ShapeDtypeStruct((B * S, D), f32),
            scratch_shapes=[
                pltpu.VMEM((FFN_TILE, D), bf16),
                pltpu.VMEM((FFN_TILE, d_ff), bf16),
                pltpu.VMEM((FFN_TILE, D), f32),
            ],
            compiler_params=pltpu.CompilerParams(
                dimension_semantics=("arbitrary",),
                vmem_limit_bytes=FFN_VMEM_BYTES),
            name="channel_mixer",
        )
        assert last, "the final rmsnorm is fused into the last layer's channel mixer"
        x = ffn(x.reshape(B * S, D), _stacked_rows(norm2_g), ffn_w1[l].astype(bf16), ffn_w3[l].astype(bf16),
                ffn_w2[l].astype(bf16), row(norm_f_g)).reshape(B, S, D)
    return x
```

```python
import functools
import math

import jax
import jax.numpy as jnp
import numpy as np
from jax import lax
from jax.experimental import pallas as pl
from jax.experimental.pallas import tpu as pltpu

EPS = 1e-6
LRU_C = 8.0

SUBLANES = 8
LANES = 128
MXU_COLS = 256

MIX_TILE = 512
FFN_TILE = 1024
CONV_ROWS = 128
CONV_LAG = 1
NORM_ROWS = 32
GATE_K = 512

MIX_VMEM_BYTES = 60 * 1024 * 1024
FFN_VMEM_BYTES = 56 * 1024 * 1024


def _sigmoid(x):
    return 0.5 * jnp.tanh(0.5 * x) + 0.5


def _silu(x):
    return x * _sigmoid(x)


def _gelu_tanh(x):
    c = math.sqrt(2.0 / math.pi)
    return 0.5 * x * (1.0 + jnp.tanh(c * (x + 0.044715 * (x * x * x))))


def _dot(a, b):
    return jnp.dot(a, b, preferred_element_type=jnp.float32)


def _gate_windows(d_rnn, block):
    starts = []
    for j in range(d_rnn // MXU_COLS):
        lo = (j * MXU_COLS // block) * block
        hi = -(-((j + 1) * MXU_COLS) // block) * block
        start = min(lo // LANES * LANES, d_rnn - GATE_K)
        assert start <= lo and hi <= start + GATE_K
        starts.append(start)
    return tuple(starts)


def _mixer_kernel(x_ref, n1g_ref, win_ref, bin_ref, cw_ref, cb_ref, lng_ref, lnb_ref, cwo_ref, cbo_ref,
                  lcw_ref, lcb_ref, wg_ref, ba_ref, bx_ref, lam_ref, lwo_ref, wmix_ref,
                  o_ref,
                  hp_scr, h_scr, u_scr, uprev_scr, c_scr, v_scr, rx_scr, rxprev_scr, xb_scr, xbb_scr,
                  y_scr, m_scr, mb_scr, op_scr, hcar_scr,
                  *, gate_starts):
    T, D = x_ref.shape
    L = T // SUBLANES
    d_conv = cwo_ref.shape[0]
    d_rnn = lwo_ref.shape[0]
    u_halo = cw_ref.shape[0] - 1
    rx_halo = lcw_ref.shape[0] - 1
    assert u_halo <= L and rx_halo <= L and L % NORM_ROWS == 0 and T % CONV_ROWS == 0
    off_glu_b = d_conv
    off_rx = 2 * d_conv
    off_rg = off_rx + d_rnn
    off_ga = off_rg + d_rnn
    off_gb = off_ga + D

    @pl.when(pl.program_id(1) == 0)
    def _():
        uprev_scr[...] = jnp.zeros(uprev_scr.shape, jnp.float32)
        rxprev_scr[...] = jnp.zeros(rxprev_scr.shape, jnp.float32)
        hcar_scr[...] = jnp.zeros(hcar_scr.shape, jnp.float32)

    def interleaved_rows(r, n):
        s, i0 = divmod(r, L)
        return pl.ds(SUBLANES * i0 + s, n, stride=SUBLANES)

    for r in range(0, T, NORM_ROWS):
        x = x_ref[r:r + NORM_ROWS, :]
        ms = jnp.mean(x * x, axis=-1, keepdims=True)
        h = x * lax.rsqrt(ms + EPS) * n1g_ref[...]
        for c in range(D // LANES):
            hp_scr[c, interleaved_rows(r, NORM_ROWS), :] = h[:, c * LANES:(c + 1) * LANES]
    for c in range(D // LANES):
        h_scr[:, c * LANES:(c + 1) * LANES] = hp_scr[c].astype(jnp.bfloat16)

    def in_proj(off, c0, width=MXU_COLS):
        cols = slice(off + c0, off + c0 + width)
        return _dot(h_scr[...], win_ref[:, cols]) + bin_ref[:, cols]

    def fill_halo(buf, prev, halo):
        n, width = SUBLANES * halo, buf.shape[1]
        tail = buf[T:T + n, :].reshape(halo, SUBLANES, width)
        sub = lax.broadcasted_iota(jnp.int32, tail.shape, 1)
        mixed = jnp.where(sub == SUBLANES - 1, prev[...].reshape(tail.shape), tail)
        buf[0:n, :] = pltpu.roll(mixed, 1, 1).reshape(n, width)
        prev[...] = tail.reshape(n, width)

    always = pl.program_id(1) >= 0

    def causal_conv(buf, w_ref, b_ref, emit):
        groups = CONV_ROWS // SUBLANES
        prev = None
        for l0 in range(0, buf.shape[1], LANES):
            lanes = slice(l0, l0 + LANES)
            for r in range(0, T, CONV_ROWS):
                acc = jnp.broadcast_to(b_ref[:, lanes], (groups, SUBLANES, LANES))
                if prev is not None:
                    acc = jnp.where(always, acc, prev)
                sums = [acc[0]]
                for k in range(w_ref.shape[0]):
                    row = r + SUBLANES * k
                    wk = jnp.broadcast_to(w_ref[k:k + 1, lanes], (SUBLANES, LANES))
                    if k >= CONV_LAG:
                        wk = jnp.where(always, wk, sums[k - CONV_LAG])
                    acc = acc + wk[None] * buf[row:row + CONV_ROWS, lanes].reshape(groups, SUBLANES, LANES)
                    sums.append(acc[0])
                emit(r, lanes, acc.reshape(CONV_ROWS, LANES))
                prev = acc

    u0 = SUBLANES * u_halo
    for c0 in range(0, d_conv, MXU_COLS):
        za = in_proj(0, c0)
        zb = in_proj(off_glu_b, c0)
        u_scr[u0:u0 + T, c0:c0 + MXU_COLS] = za * _sigmoid(zb)
    fill_halo(u_scr, uprev_scr, u_halo)

    def emit_conv(r, lanes, acc):
        c_scr[r:r + CONV_ROWS, lanes] = acc
    causal_conv(u_scr, cw_ref, cb_ref, emit_conv)

    for r in range(0, T, NORM_ROWS):
        cv = c_scr[r:r + NORM_ROWS, :]
        mu = jnp.mean(cv, axis=-1, keepdims=True)
        xc = cv - mu
        var = jnp.mean(xc * xc, axis=-1, keepdims=True)
        y = xc * lax.rsqrt(var + EPS) * lng_ref[...] + lnb_ref[...]
        v_scr[r:r + NORM_ROWS, :] = _silu(y).astype(jnp.bfloat16)

    for c0 in range(0, D, MXU_COLS):
        cols = slice(c0, c0 + MXU_COLS)
        ya = _dot(v_scr[...], cwo_ref[:, cols]) + cbo_ref[:, cols]
        m_scr[:, cols] = _sigmoid(in_proj(off_ga, c0)) * ya

    rx0 = SUBLANES * rx_halo
    for c0 in range(0, d_rnn, MXU_COLS):
        rx_scr[rx0:rx0 + T, c0:c0 + MXU_COLS] = in_proj(off_rx, c0)
    fill_halo(rx_scr, rxprev_scr, rx_halo)

    def emit_rx(r, lanes, acc):
        xb_scr[r:r + CONV_ROWS, lanes] = acc
        xbb_scr[r:r + CONV_ROWS, lanes] = acc.astype(jnp.bfloat16)
    causal_conv(rx_scr, lcw_ref, lcb_ref, emit_rx)

    lam = lam_ref[...]
    neg_c_sp = -LRU_C * (jnp.maximum(-lam, 0.0) + jnp.log1p(jnp.exp(-jnp.abs(lam))))

    sub = lax.broadcasted_iota(jnp.int32, (SUBLANES, MXU_COLS), 0)
    for j, ks in enumerate(gate_starts):
        c0 = j * MXU_COLS
        cols = slice(c0, c0 + MXU_COLS)
        g = _dot(xbb_scr[:, ks:ks + GATE_K], wg_ref[j])
        r_gate = _sigmoid(g[:, :MXU_COLS] + ba_ref[:, cols])
        i_gate = _sigmoid(g[:, MXU_COLS:] + bx_ref[:, cols])
        log_a = neg_c_sp[:, cols] * r_gate
        a = jnp.exp(log_a)
        mult = jnp.sqrt(-jnp.tanh(log_a) * (1.0 + a * a))
        b = mult * (i_gate * xb_scr[:, cols])

        a3 = a.reshape(L, SUBLANES, MXU_COLS)
        b3 = b.reshape(L, SUBLANES, MXU_COLS)
        eb, ea = b3[0], a3[0]
        for i in range(1, L):
            eb = a3[i] * eb + b3[i]
            ea = a3[i] * ea
        d = 1
        while d < SUBLANES:
            keep = sub >= d
            ea_sh = jnp.where(keep, pltpu.roll(ea, d, 0), 1.0)
            eb_sh = jnp.where(keep, pltpu.roll(eb, d, 0), 0.0)
            eb = eb + ea * eb_sh
            ea = ea * ea_sh
            d *= 2
        h_in = hcar_scr[:, cols]
        ends = eb + ea * h_in
        start = jnp.where(sub == 0, h_in, pltpu.roll(ends, 1, 0))
        hcar_scr[:, cols] = jnp.broadcast_to(ends[SUBLANES - 1:SUBLANES, :], (SUBLANES, MXU_COLS))
        hs, h = [], start
        for i in range(L):
            h = a3[i] * h + b3[i]
            hs.append(h)
        hseq = jnp.concatenate(hs, axis=0)

        y_scr[:, cols] = (hseq * _gelu_tanh(in_proj(off_rg, c0))).astype(jnp.bfloat16)

    for c0 in range(0, D, MXU_COLS):
        cols = slice(c0, c0 + MXU_COLS)
        yb = _dot(y_scr[...], lwo_ref[:, cols])
        mb_scr[:, cols] = (m_scr[:, cols] + _sigmoid(in_proj(off_gb, c0)) * yb).astype(jnp.bfloat16)

    for c0 in range(0, D, MXU_COLS):
        res = _dot(mb_scr[...], wmix_ref[:, c0:c0 + MXU_COLS])
        for q in range(MXU_COLS // LANES):
            op_scr[c0 // LANES + q] = res[:, q * LANES:(q + 1) * LANES]
    for r in range(0, T, NORM_ROWS):
        for c in range(D // LANES):
            lanes = slice(c * LANES, (c + 1) * LANES)
            o_ref[r:r + NORM_ROWS, lanes] = (x_ref[r:r + NORM_ROWS, lanes]
                                             + op_scr[c, interleaved_rows(r, NORM_ROWS), :])


def _ffn_kernel(x_ref, n2g_ref, w1_ref, w3_ref, w2_ref, nfg_ref, o_ref, h_scr, act_scr, y_scr):
    tm, D = x_ref.shape
    d_ff = w1_ref.shape[1]

    for r in range(0, tm, NORM_ROWS):
        x = x_ref[r:r + NORM_ROWS, :]
        ms = jnp.mean(x * x, axis=-1, keepdims=True)
        h_scr[r:r + NORM_ROWS, :] = (x * lax.rsqrt(ms + EPS) * n2g_ref[...]).astype(jnp.bfloat16)

    for c0 in range(0, d_ff, MXU_COLS):
        cols = slice(c0, c0 + MXU_COLS)
        g = _dot(h_scr[...], w1_ref[:, cols])
        u = _dot(h_scr[...], w3_ref[:, cols])
        act_scr[:, cols] = (_silu(g) * u).astype(jnp.bfloat16)

    for c0 in range(0, D, MXU_COLS):
        cols = slice(c0, c0 + MXU_COLS)
        y_scr[:, cols] = x_ref[:, cols] + _dot(act_scr[...], w2_ref[:, cols])

    for r in range(0, tm, NORM_ROWS):
        y = y_scr[r:r + NORM_ROWS, :]
        ms = jnp.mean(y * y, axis=-1, keepdims=True)
        o_ref[r:r + NORM_ROWS, :] = y * lax.rsqrt(ms + EPS) * nfg_ref[...]


def _resident(shape):
    return pl.BlockSpec(shape, lambda *_: (0,) * len(shape), pipeline_mode=pl.Buffered(1))


def _layer_resident(shape, layer):
    return pl.BlockSpec((None,) + tuple(shape), lambda *_: (layer,) + (0,) * len(shape),
                        pipeline_mode=pl.Buffered(1))


def _stacked_rows(p):
    return p.reshape(p.shape[0], 1, p.shape[1])


def _gate_weights(wa, wx, gate_starts):
    heads, blk, _ = wa.shape
    d_rnn = heads * blk
    head_of = np.arange(d_rnn) // blk
    same_head = jnp.asarray(head_of[:, None] == head_of[None, :])

    def dense(w):
        return jnp.where(same_head, jnp.tile(w.reshape(d_rnn, blk), (1, heads)), 0.0)

    da, dx = dense(wa), dense(wx)
    tiles = []
    for j, ks in enumerate(gate_starts):
        cols = slice(j * MXU_COLS, (j + 1) * MXU_COLS)
        tiles.append(jnp.concatenate([da[ks:ks + GATE_K, cols], dx[ks:ks + GATE_K, cols]], axis=1))
    return jnp.stack(tiles).astype(jnp.bfloat16)


def kernel(x, norm1_g, w_in, b_in, conv_dw_w, conv_dw_b, conv_ln_g, conv_ln_b, conv_w_out, conv_b_out,
           lru_conv_w, lru_conv_b, lru_wa, lru_ba, lru_wx, lru_bx, lru_lambda, lru_w_out, w_mix_out,
           norm2_g, ffn_w1, ffn_w3, ffn_w2, norm_f_g):
    B, S, D = x.shape
    depth = w_in.shape[0]
    d_in = w_in.shape[2]
    conv_k, d_conv = conv_dw_w.shape[1:]
    lru_k, d_rnn = lru_conv_w.shape[1:]
    heads, blk = lru_wa.shape[1:3]
    d_ff = ffn_w1.shape[2]
    T = MIX_TILE
    assert S % T == 0 and (B * S) % FFN_TILE == 0
    assert d_in == 2 * d_conv + 2 * d_rnn + 2 * D and heads * blk == d_rnn
    assert d_conv % MXU_COLS == 0 and d_rnn % MXU_COLS == 0 and D % MXU_COLS == 0 and d_ff % MXU_COLS == 0
    u_halo_rows = SUBLANES * (conv_k - 1)
    rx_halo_rows = SUBLANES * (lru_k - 1)
    gate_starts = _gate_windows(d_rnn, blk)
    n_gate = len(gate_starts)
    bf16 = jnp.bfloat16
    f32 = jnp.float32

    def row(v):
        return v.reshape(1, -1)

    for l in range(depth):
        wg = _gate_weights(lru_wa[l], lru_wx[l], gate_starts)
        mixer = pl.pallas_call(
            functools.partial(_mixer_kernel, gate_starts=gate_starts),
            grid=(B, S // T),
            in_specs=[
                pl.BlockSpec((None, T, D), lambda b, s: (b, s, 0)),
                _layer_resident((1, D), l),
                _resident((D, d_in)), _layer_resident((1, d_in), l),
                _layer_resident((conv_k, d_conv), l), _layer_resident((1, d_conv), l),
                _layer_resident((1, d_conv), l), _layer_resident((1, d_conv), l),
                _resident((d_conv, D)), _layer_resident((1, D), l),
                _layer_resident((lru_k, d_rnn), l), _layer_resident((1, d_rnn), l),
                _resident((n_gate, GATE_K, 2 * MXU_COLS)),
                _layer_resident((1, d_rnn), l), _layer_resident((1, d_rnn), l), _layer_resident((1, d_rnn), l),
                _resident((d_rnn, D)), _resident((D, D)),
            ],
            out_specs=pl.BlockSpec((None, T, D), lambda b, s: (b, s, 0)),
            out_shape=jax.ShapeDtypeStruct((B, S, D), f32),
            scratch_shapes=[
                pltpu.VMEM((D // LANES, T, LANES), f32),
                pltpu.VMEM((T, D), bf16),
                pltpu.VMEM((u_halo_rows + T, d_conv), f32),
                pltpu.VMEM((u_halo_rows, d_conv), f32),
                pltpu.VMEM((T, d_conv), f32),
                pltpu.VMEM((T, d_conv), bf16),
                pltpu.VMEM((rx_halo_rows + T, d_rnn), f32),
                pltpu.VMEM((rx_halo_rows, d_rnn), f32),
                pltpu.VMEM((T, d_rnn), f32),
                pltpu.VMEM((T, d_rnn), bf16),
                pltpu.VMEM((T, d_rnn), bf16),
                pltpu.VMEM((T, D), f32),
                pltpu.VMEM((T, D), bf16),
                pltpu.VMEM((D // LANES, T, LANES), f32),
                pltpu.VMEM((SUBLANES, d_rnn), f32),
            ],
            compiler_params=pltpu.CompilerParams(
                dimension_semantics=("arbitrary", "arbitrary"),
                vmem_limit_bytes=MIX_VMEM_BYTES),
            name="token_mixer",
        )
        x = mixer(
            x, _stacked_rows(norm1_g), w_in[l].astype(bf16), _stacked_rows(b_in),
            conv_dw_w, _stacked_rows(conv_dw_b), _stacked_rows(conv_ln_g), _stacked_rows(conv_ln_b),
            conv_w_out[l].astype(bf16), _stacked_rows(conv_b_out),
            lru_conv_w, _stacked_rows(lru_conv_b), wg, _stacked_rows(lru_ba), _stacked_rows(lru_bx),
            _stacked_rows(lru_lambda),
            lru_w_out[l].astype(bf16), w_mix_out[l].astype(bf16))

        last = l == depth - 1
        ffn = pl.pallas_call(
            _ffn_kernel,
            grid=(B * S // FFN_TILE,),
            in_specs=[
                pl.BlockSpec((FFN_TILE, D), lambda i: (i, 0)),
                _layer_resident((1, D), l),
                _resident((D, d_ff)), _resident((D, d_ff)), _resident((d_ff, D)),
                _resident((1, D)),
            ],
            out_specs=pl.BlockSpec((FFN_TILE, D), lambda i: (i, 0)),
            out_shape=jax.ShapeDtypeStruct((B * S, D), f32),
            scratch_shapes=[
                pltpu.VMEM((FFN_TILE, D), bf16),
                pltpu.VMEM((FFN_TILE, d_ff), bf16),
                pltpu.VMEM((FFN_TILE, D), f32),
            ],
            compiler_params=pltpu.CompilerParams(
                dimension_semantics=("arbitrary",),
                vmem_limit_bytes=FFN_VMEM_BYTES),
            name="channel_mixer",
        )
        assert last, "the final rmsnorm is fused into the last layer's channel mixer"
        x = ffn(x.reshape(B * S, D), _stacked_rows(norm2_g), ffn_w1[l].astype(bf16), ffn_w3[l].astype(bf16),
                ffn_w2[l].astype(bf16), row(norm_f_g)).reshape(B, S, D)
    return x
```

```python
import functools
import math

import jax
import jax.numpy as jnp
import numpy as np
from jax import lax
from jax.experimental import pallas as pl
from jax.experimental.pallas import tpu as pltpu

EPS = 1e-6
LRU_C = 8.0

SUBLANES = 8
LANES = 128
MXU_COLS = 256

MIX_TILE = 512
FFN_TILE = 1024
CONV_ROWS = 128
CONV_LAG = 1
NORM_ROWS = 32
GATE_K = 512

MIX_VMEM_BYTES = 60 * 1024 * 1024
FFN_VMEM_BYTES = 56 * 1024 * 1024


def _sigmoid(x):
    return 0.5 * jnp.tanh(0.5 * x) + 0.5


def _silu(x):
    return x * _sigmoid(x)


def _gelu_tanh(x):
    c = math.sqrt(2.0 / math.pi)
    return 0.5 * x * (1.0 + jnp.tanh(c * (x + 0.044715 * (x * x * x))))


def _dot(a, b):
    return jnp.dot(a, b, preferred_element_type=jnp.float32)


def _gate_windows(d_rnn, block):
    starts = []
    for j in range(d_rnn // MXU_COLS):
        lo = (j * MXU_COLS // block) * block
        hi = -(-((j + 1) * MXU_COLS) // block) * block
        start = min(lo // LANES * LANES, d_rnn - GATE_K)
        assert start <= lo and hi <= start + GATE_K
        starts.append(start)
    return tuple(starts)


def _mixer_kernel(x_ref, n1g_ref, win_ref, bin_ref, cw_ref, cb_ref, lng_ref, lnb_ref, cwo_ref, cbo_ref,
                  lcw_ref, lcb_ref, wg_ref, ba_ref, bx_ref, lam_ref, lwo_ref, wmix_ref,
                  o_ref,
                  hp_scr, h_scr, u_scr, uprev_scr, c_scr, v_scr, rx_scr, rxprev_scr, xb_scr, xbb_scr,
                  y_scr, m_scr, mb_scr, op_scr, hcar_scr,
                  *, gate_starts):
    T, D = x_ref.shape
    L = T // SUBLANES
    d_conv = cwo_ref.shape[0]
    d_rnn = lwo_ref.shape[0]
    u_halo = cw_ref.shape[0] - 1
    rx_halo = lcw_ref.shape[0] - 1
    assert u_halo <= L and rx_halo <= L and L % NORM_ROWS == 0 and T % CONV_ROWS == 0
    off_rx = 2 * d_conv
    off_rg = off_rx + d_rnn
    off_ga = off_rg + d_rnn
    off_gb = off_ga + D

    @pl.when(pl.program_id(1) == 0)
    def _():
        uprev_scr[...] = jnp.zeros(uprev_scr.shape, jnp.float32)
        rxprev_scr[...] = jnp.zeros(rxprev_scr.shape, jnp.float32)
        hcar_scr[...] = jnp.zeros(hcar_scr.shape, jnp.float32)

    def interleaved_rows(r, n):
        s, i0 = divmod(r, L)
        return pl.ds(SUBLANES * i0 + s, n, stride=SUBLANES)

    for r in range(0, T, NORM_ROWS):
        x = x_ref[r:r + NORM_ROWS, :]
        ms = jnp.mean(x * x, axis=-1, keepdims=True)
        h = x * lax.rsqrt(ms + EPS) * n1g_ref[...]
        for c in range(D // LANES):
            hp_scr[c, interleaved_rows(r, NORM_ROWS), :] = h[:, c * LANES:(c + 1) * LANES]
    for c in range(D // LANES):
        h_scr[:, c * LANES:(c + 1) * LANES] = hp_scr[c].astype(jnp.bfloat16)

    def in_proj(off, c0, width=MXU_COLS):
        cols = slice(off + c0, off + c0 + width)
        return _dot(h_scr[...], win_ref[:, cols]) + bin_ref[:, cols]

    def fill_halo(buf, prev, halo):
        n, width = SUBLANES * halo, buf.shape[1]
        tail = buf[T:T + n, :].reshape(halo, SUBLANES, width)
        sub = lax.broadcasted_iota(jnp.int32, tail.shape, 1)
        mixed = jnp.where(sub == SUBLANES - 1, prev[...].reshape(tail.shape), tail)
        buf[0:n, :] = pltpu.roll(mixed, 1, 1).reshape(n, width)
        prev[...] = tail.reshape(n, width)

    always = pl.program_id(1) >= 0

    def causal_conv(buf, w_ref, b_ref, emit):
        groups = CONV_ROWS // SUBLANES
        prev = None
        for l0 in range(0, buf.shape[1], LANES):
            lanes = slice(l0, l0 + LANES)
            for r in range(0, T, CONV_ROWS):
                acc = jnp.broadcast_to(b_ref[:, lanes], (groups, SUBLANES, LANES))
                if prev is not None:
                    acc = jnp.where(always, acc, prev)
                sums = [acc[0]]
                for k in range(w_ref.shape[0]):
                    row = r + SUBLANES * k
                    wk = jnp.broadcast_to(w_ref[k:k + 1, lanes], (SUBLANES, LANES))
                    if k >= CONV_LAG:
                        wk = jnp.where(always, wk, sums[k - CONV_LAG])
                    acc = acc + wk[None] * buf[row:row + CONV_ROWS, lanes].reshape(groups, SUBLANES, LANES)
                    sums.append(acc[0])
                emit(r, lanes, acc.reshape(CONV_ROWS, LANES))
                prev = acc

    u0 = SUBLANES * u_halo
    for k in range(d_conv // LANES):
        z = in_proj(0, k * MXU_COLS)
        u_scr[u0:u0 + T, k * LANES:(k + 1) * LANES] = z[:, :LANES] * _sigmoid(z[:, LANES:])
    fill_halo(u_scr, uprev_scr, u_halo)

    def emit_conv(r, lanes, acc):
        c_scr[r:r + CONV_ROWS, lanes] = acc
    causal_conv(u_scr, cw_ref, cb_ref, emit_conv)

    for r in range(0, T, NORM_ROWS):
        cv = c_scr[r:r + NORM_ROWS, :]
        mu = jnp.mean(cv, axis=-1, keepdims=True)
        xc = cv - mu
        var = jnp.mean(xc * xc, axis=-1, keepdims=True)
        y = xc * lax.rsqrt(var + EPS) * lng_ref[...] + lnb_ref[...]
        v_scr[r:r + NORM_ROWS, :] = _silu(y).astype(jnp.bfloat16)

    for c0 in range(0, D, MXU_COLS):
        cols = slice(c0, c0 + MXU_COLS)
        ya = _dot(v_scr[...], cwo_ref[:, cols]) + cbo_ref[:, cols]
        m_scr[:, cols] = _sigmoid(in_proj(off_ga, c0)) * ya

    rx0 = SUBLANES * rx_halo
    for c0 in range(0, d_rnn, MXU_COLS):
        rx_scr[rx0:rx0 + T, c0:c0 + MXU_COLS] = in_proj(off_rx, c0)
    fill_halo(rx_scr, rxprev_scr, rx_halo)

    def emit_rx(r, lanes, acc):
        xb_scr[r:r + CONV_ROWS, lanes] = acc
        xbb_scr[r:r + CONV_ROWS, lanes] = acc.astype(jnp.bfloat16)
    causal_conv(rx_scr, lcw_ref, lcb_ref, emit_rx)

    lam = lam_ref[...]
    neg_c_sp = -LRU_C * (jnp.maximum(-lam, 0.0) + jnp.log1p(jnp.exp(-jnp.abs(lam))))

    sub = lax.broadcasted_iota(jnp.int32, (SUBLANES, MXU_COLS), 0)
    for j, ks in enumerate(gate_starts):
        c0 = j * MXU_COLS
        cols = slice(c0, c0 + MXU_COLS)
        g = _dot(xbb_scr[:, ks:ks + GATE_K], wg_ref[j])
        r_gate = _sigmoid(g[:, :MXU_COLS] + ba_ref[:, cols])
        i_gate = _sigmoid(g[:, MXU_COLS:] + bx_ref[:, cols])
        log_a = neg_c_sp[:, cols] * r_gate
        a = jnp.exp(log_a)
        mult = jnp.sqrt(-jnp.tanh(log_a) * (1.0 + a * a))
        b = mult * (i_gate * xb_scr[:, cols])

        a3 = a.reshape(L, SUBLANES, MXU_COLS)
        b3 = b.reshape(L, SUBLANES, MXU_COLS)
        hloc, prod = [b3[0]], [a3[0]]
        for i in range(1, L):
            hloc.append(a3[i] * hloc[-1] + b3[i])
            prod.append(a3[i] * prod[-1])
        ea, eb = prod[-1], hloc[-1]
        d = 1
        while d < SUBLANES:
            keep = sub >= d
            ea_sh = jnp.where(keep, pltpu.roll(ea, d, 0), 1.0)
            eb_sh = jnp.where(keep, pltpu.roll(eb, d, 0), 0.0)
            eb = eb + ea * eb_sh
            ea = ea * ea_sh
            d *= 2
        h_in = hcar_scr[:, cols]
        ends = eb + ea * h_in
        start = jnp.where(sub == 0, h_in, pltpu.roll(ends, 1, 0))
        hcar_scr[:, cols] = jnp.broadcast_to(ends[SUBLANES - 1:SUBLANES, :], (SUBLANES, MXU_COLS))
        hseq = jnp.concatenate([hl + pr * start for hl, pr in zip(hloc, prod)], axis=0)

        y_scr[:, cols] = (hseq * _gelu_tanh(in_proj(off_rg, c0))).astype(jnp.bfloat16)

    for c0 in range(0, D, MXU_COLS):
        cols = slice(c0, c0 + MXU_COLS)
        yb = _dot(y_scr[...], lwo_ref[:, cols])
        mb_scr[:, cols] = (m_scr[:, cols] + _sigmoid(in_proj(off_gb, c0)) * yb).astype(jnp.bfloat16)

    for c0 in range(0, D, MXU_COLS):
        res = _dot(mb_scr[...], wmix_ref[:, c0:c0 + MXU_COLS])
        for q in range(MXU_COLS // LANES):
            op_scr[c0 // LANES + q] = res[:, q * LANES:(q + 1) * LANES]
    for r in range(0, T, NORM_ROWS):
        for c in range(D // LANES):
            lanes = slice(c * LANES, (c + 1) * LANES)
            o_ref[r:r + NORM_ROWS, lanes] = (x_ref[r:r + NORM_ROWS, lanes]
                                             + op_scr[c, interleaved_rows(r, NORM_ROWS), :])


def _ffn_kernel(x_ref, n2g_ref, w1_ref, w3_ref, w2_ref, nfg_ref, o_ref, h_scr, act_scr, y_scr):
    tm, D = x_ref.shape
    d_ff = w1_ref.shape[1]

    for r in range(0, tm, NORM_ROWS):
        x = x_ref[r:r + NORM_ROWS, :]
        ms = jnp.mean(x * x, axis=-1, keepdims=True)
        h_scr[r:r + NORM_ROWS, :] = (x * lax.rsqrt(ms + EPS) * n2g_ref[...]).astype(jnp.bfloat16)

    for c0 in range(0, d_ff, MXU_COLS):
        cols = slice(c0, c0 + MXU_COLS)
        g = _dot(h_scr[...], w1_ref[:, cols])
        u = _dot(h_scr[...], w3_ref[:, cols])
        act_scr[:, cols] = (_silu(g) * u).astype(jnp.bfloat16)

    for c0 in range(0, D, MXU_COLS):
        cols = slice(c0, c0 + MXU_COLS)
        y_scr[:, cols] = x_ref[:, cols] + _dot(act_scr[...], w2_ref[:, cols])

    for r in range(0, tm, NORM_ROWS):
        y = y_scr[r:r + NORM_ROWS, :]
        ms = jnp.mean(y * y, axis=-1, keepdims=True)
        o_ref[r:r + NORM_ROWS, :] = y * lax.rsqrt(ms + EPS) * nfg_ref[...]


def _resident(shape):
    return pl.BlockSpec(shape, lambda *_: (0,) * len(shape), pipeline_mode=pl.Buffered(1))


def _layer_resident(shape, layer):
    return pl.BlockSpec((None,) + tuple(shape), lambda *_: (layer,) + (0,) * len(shape),
                        pipeline_mode=pl.Buffered(1))


def _interleave_glu(w, d_conv):
    lead = w.shape[:-1]
    glu = w[..., :2 * d_conv].reshape(*lead, 2, d_conv // LANES, LANES)
    glu = jnp.swapaxes(glu, -3, -2).reshape(*lead, 2 * d_conv)
    return jnp.concatenate([glu, w[..., 2 * d_conv:]], axis=-1)


def _stacked_rows(p):
    return p.reshape(p.shape[0], 1, p.shape[1])


def _gate_weights(wa, wx, gate_starts):
    heads, blk, _ = wa.shape
    d_rnn = heads * blk
    head_of = np.arange(d_rnn) // blk
    same_head = jnp.asarray(head_of[:, None] == head_of[None, :])

    def dense(w):
        return jnp.where(same_head, jnp.tile(w.reshape(d_rnn, blk), (1, heads)), 0.0)

    da, dx = dense(wa), dense(wx)
    tiles = []
    for j, ks in enumerate(gate_starts):
        cols = slice(j * MXU_COLS, (j + 1) * MXU_COLS)
        tiles.append(jnp.concatenate([da[ks:ks + GATE_K, cols], dx[ks:ks + GATE_K, cols]], axis=1))
    return jnp.stack(tiles).astype(jnp.bfloat16)


def kernel(x, norm1_g, w_in, b_in, conv_dw_w, conv_dw_b, conv_ln_g, conv_ln_b, conv_w_out, conv_b_out,
           lru_conv_w, lru_conv_b, lru_wa, lru_ba, lru_wx, lru_bx, lru_lambda, lru_w_out, w_mix_out,
           norm2_g, ffn_w1, ffn_w3, ffn_w2, norm_f_g):
    B, S, D = x.shape
    depth = w_in.shape[0]
    d_in = w_in.shape[2]
    conv_k, d_conv = conv_dw_w.shape[1:]
    lru_k, d_rnn = lru_conv_w.shape[1:]
    heads, blk = lru_wa.shape[1:3]
    d_ff = ffn_w1.shape[2]
    T = MIX_TILE
    assert S % T == 0 and (B * S) % FFN_TILE == 0
    assert d_in == 2 * d_conv + 2 * d_rnn + 2 * D and heads * blk == d_rnn
    assert d_conv % MXU_COLS == 0 and d_rnn % MXU_COLS == 0 and D % MXU_COLS == 0 and d_ff % MXU_COLS == 0
    u_halo_rows = SUBLANES * (conv_k - 1)
    rx_halo_rows = SUBLANES * (lru_k - 1)
    gate_starts = _gate_windows(d_rnn, blk)
    n_gate = len(gate_starts)
    bf16 = jnp.bfloat16
    f32 = jnp.float32

    def row(v):
        return v.reshape(1, -1)

    for l in range(depth):
        wg = _gate_weights(lru_wa[l], lru_wx[l], gate_starts)
        mixer = pl.pallas_call(
            functools.partial(_mixer_kernel, gate_starts=gate_starts),
            grid=(B, S // T),
            in_specs=[
                pl.BlockSpec((None, T, D), lambda b, s: (b, s, 0)),
                _layer_resident((1, D), l),
                _resident((D, d_in)), _layer_resident((1, d_in), l),
                _layer_resident((conv_k, d_conv), l), _layer_resident((1, d_conv), l),
                _layer_resident((1, d_conv), l), _layer_resident((1, d_conv), l),
                _resident((d_conv, D)), _layer_resident((1, D), l),
                _layer_resident((lru_k, d_rnn), l), _layer_resident((1, d_rnn), l),
                _resident((n_gate, GATE_K, 2 * MXU_COLS)),
                _layer_resident((1, d_rnn), l), _layer_resident((1, d_rnn), l), _layer_resident((1, d_rnn), l),
                _resident((d_rnn, D)), _resident((D, D)),
            ],
            out_specs=pl.BlockSpec((None, T, D), lambda b, s: (b, s, 0)),
            out_shape=jax.ShapeDtypeStruct((B, S, D), f32),
            scratch_shapes=[
                pltpu.VMEM((D // LANES, T, LANES), f32),
                pltpu.VMEM((T, D), bf16),
                pltpu.VMEM((u_halo_rows + T, d_conv), f32),
                pltpu.VMEM((u_halo_rows, d_conv), f32),
                pltpu.VMEM((T, d_conv), f32),
                pltpu.VMEM((T, d_conv), bf16),
                pltpu.VMEM((rx_halo_rows + T, d_rnn), f32),
                pltpu.VMEM((rx_halo_rows, d_rnn), f32),
                pltpu.VMEM((T, d_rnn), f32),
                pltpu.VMEM((T, d_rnn), bf16),
                pltpu.VMEM((T, d_rnn), bf16),
                pltpu.VMEM((T, D), f32),
                pltpu.VMEM((T, D), bf16),
                pltpu.VMEM((D // LANES, T, LANES), f32),
                pltpu.VMEM((SUBLANES, d_rnn), f32),
            ],
            compiler_params=pltpu.CompilerParams(
                dimension_semantics=("arbitrary", "arbitrary"),
                vmem_limit_bytes=MIX_VMEM_BYTES),
            name="token_mixer",
        )
        x = mixer(
            x, _stacked_rows(norm1_g), _interleave_glu(w_in[l], d_conv).astype(bf16),
            _stacked_rows(_interleave_glu(b_in, d_conv)),
            conv_dw_w, _stacked_rows(conv_dw_b), _stacked_rows(conv_ln_g), _stacked_rows(conv_ln_b),
            conv_w_out[l].astype(bf16), _stacked_rows(conv_b_out),
            lru_conv_w, _stacked_rows(lru_conv_b), wg, _stacked_rows(lru_ba), _stacked_rows(lru_bx),
            _stacked_rows(lru_lambda),
            lru_w_out[l].astype(bf16), w_mix_out[l].astype(bf16))

        last = l == depth - 1
        ffn = pl.pallas_call(
            _ffn_kernel,
            grid=(B * S // FFN_TILE,),
            in_specs=[
                pl.BlockSpec((FFN_TILE, D), lambda i: (i, 0)),
                _layer_resident((1, D), l),
                _resident((D, d_ff)), _resident((D, d_ff)), _resident((d_ff, D)),
                _resident((1, D)),
            ],
            out_specs=pl.BlockSpec((FFN_TILE, D), lambda i: (i, 0)),
            out_shape=jax.ShapeDtypeStruct((B * S, D), f32),
            scratch_shapes=[
                pltpu.VMEM((FFN_TILE, D), bf16),
                pltpu.VMEM((FFN_TILE, d_ff), bf16),
                pltpu.VMEM((FFN_TILE, D), f32),
            ],
            compiler_params=pltpu.CompilerParams(
                dimension_semantics=("arbitrary",),
                vmem_limit_bytes=FFN_VMEM_BYTES),
            name="channel_mixer",
        )
        assert last, "the final rmsnorm is fused into the last layer's channel mixer"
        x = ffn(x.reshape(B * S, D), _stacked_rows(norm2_g), ffn_w1[l].astype(bf16), ffn_w3[l].astype(bf16),
                ffn_w2[l].astype(bf16), row(norm_f_g)).reshape(B, S, D)
    return x
```

```python
import functools
import math

import jax
import jax.numpy as jnp
import numpy as np
from jax import lax
from jax.experimental import pallas as pl
from jax.experimental.pallas import tpu as pltpu

EPS = 1e-6
LRU_C = 8.0

SUBLANES = 8
LANES = 128
MXU_COLS = 256

MIX_TILE = 512
FFN_TILE = 1024
CONV_ROWS = 128
CONV_LAG = 1
NORM_ROWS = 32
GATE_K = 512

MIX_VMEM_BYTES = 60 * 1024 * 1024
FFN_VMEM_BYTES = 56 * 1024 * 1024


def _sigmoid(x):
    return 0.5 * jnp.tanh(0.5 * x) + 0.5


def _silu(x):
    return x * _sigmoid(x)


def _gelu_tanh(x):
    c = math.sqrt(2.0 / math.pi)
    return 0.5 * x * (1.0 + jnp.tanh(c * (x + 0.044715 * (x * x * x))))


def _dot(a, b):
    return jnp.dot(a, b, preferred_element_type=jnp.float32)


def _gate_windows(d_rnn, block):
    starts = []
    for j in range(d_rnn // MXU_COLS):
        lo = (j * MXU_COLS // block) * block
        hi = -(-((j + 1) * MXU_COLS) // block) * block
        start = min(lo // LANES * LANES, d_rnn - GATE_K)
        assert start <= lo and hi <= start + GATE_K
        starts.append(start)
    return tuple(starts)


def _mixer_kernel(x_ref, n1g_ref, win_ref, bin_ref, cw_ref, cb_ref, lng_ref, lnb_ref, cwo_ref, cbo_ref,
                  lcw_ref, lcb_ref, wg_ref, ba_ref, bx_ref, lam_ref, lwo_ref, wmix_ref,
                  o_ref,
                  hp_scr, h_scr, u_scr, uprev_scr, c_scr, v_scr, rx_scr, rxprev_scr, xb_scr, xbb_scr,
                  y_scr, m_scr, mb_scr, op_scr, hcar_scr,
                  *, gate_starts):
    T, D = x_ref.shape
    L = T // SUBLANES
    d_conv = cwo_ref.shape[0]
    d_rnn = lwo_ref.shape[0]
    u_halo = cw_ref.shape[0] - 1
    rx_halo = lcw_ref.shape[0] - 1
    assert u_halo <= L and rx_halo <= L and L % NORM_ROWS == 0 and T % CONV_ROWS == 0
    off_glu_b = d_conv
    off_rx = 2 * d_conv
    off_rg = off_rx + d_rnn
    off_ga = off_rg + d_rnn
    off_gb = off_ga + D

    @pl.when(pl.program_id(1) == 0)
    def _():
        uprev_scr[...] = jnp.zeros(uprev_scr.shape, jnp.float32)
        rxprev_scr[...] = jnp.zeros(rxprev_scr.shape, jnp.float32)
        hcar_scr[...] = jnp.zeros(hcar_scr.shape, jnp.float32)

    def interleaved_rows(r, n):
        s, i0 = divmod(r, L)
        return pl.ds(SUBLANES * i0 + s, n, stride=SUBLANES)

    for r in range(0, T, NORM_ROWS):
        x = x_ref[r:r + NORM_ROWS, :]
        ms = jnp.mean(x * x, axis=-1, keepdims=True)
        h = x * lax.rsqrt(ms + EPS) * n1g_ref[...]
        for c in range(D // LANES):
            hp_scr[c, interleaved_rows(r, NORM_ROWS), :] = h[:, c * LANES:(c + 1) * LANES]
    for c in range(D // LANES):
        h_scr[:, c * LANES:(c + 1) * LANES] = hp_scr[c].astype(jnp.bfloat16)

    def in_proj(off, c0, width=MXU_COLS):
        cols = slice(off + c0, off + c0 + width)
        return _dot(h_scr[...], win_ref[:, cols]) + bin_ref[:, cols]

    def fill_halo(buf, prev, halo):
        n, width = SUBLANES * halo, buf.shape[1]
        tail = buf[T:T + n, :].reshape(halo, SUBLANES, width)
        sub = lax.broadcasted_iota(jnp.int32, tail.shape, 1)
        mixed = jnp.where(sub == SUBLANES - 1, prev[...].reshape(tail.shape), tail)
        buf[0:n, :] = pltpu.roll(mixed, 1, 1).reshape(n, width)
        prev[...] = tail.reshape(n, width)

    always = pl.program_id(1) >= 0

    def causal_conv(buf, w_ref, b_ref, emit):
        groups = CONV_ROWS // SUBLANES
        prev = None
        for l0 in range(0, buf.shape[1], LANES):
            lanes = slice(l0, l0 + LANES)
            for r in range(0, T, CONV_ROWS):
                acc = jnp.broadcast_to(b_ref[:, lanes], (groups, SUBLANES, LANES))
                if prev is not None:
                    acc = jnp.where(always, acc, prev)
                sums = [acc[0]]
                for k in range(w_ref.shape[0]):
                    row = r + SUBLANES * k
                    wk = jnp.broadcast_to(w_ref[k:k + 1, lanes], (SUBLANES, LANES))
                    if k >= CONV_LAG:
                        wk = jnp.where(always, wk, sums[k - CONV_LAG])
                    acc = acc + wk[None] * buf[row:row + CONV_ROWS, lanes].reshape(groups, SUBLANES, LANES)
                    sums.append(acc[0])
                emit(r, lanes, acc.reshape(CONV_ROWS, LANES))
                prev = acc

    u0 = SUBLANES * u_halo
    for c0 in range(0, d_conv, MXU_COLS):
        za = in_proj(0, c0)
        zb = in_proj(off_glu_b, c0)
        u_scr[u0:u0 + T, c0:c0 + MXU_COLS] = za * _sigmoid(zb)
    fill_halo(u_scr, uprev_scr, u_halo)

    def emit_conv(r, lanes, acc):
        c_scr[r:r + CONV_ROWS, lanes] = acc
    causal_conv(u_scr, cw_ref, cb_ref, emit_conv)

    for r in range(0, T, NORM_ROWS):
        cv = c_scr[r:r + NORM_ROWS, :]
        mu = jnp.mean(cv, axis=-1, keepdims=True)
        xc = cv - mu
        var = jnp.mean(xc * xc, axis=-1, keepdims=True)
        y = xc * lax.rsqrt(var + EPS) * lng_ref[...] + lnb_ref[...]
        v_scr[r:r + NORM_ROWS, :] = _silu(y).astype(jnp.bfloat16)

    for c0 in range(0, D, MXU_COLS):
        cols = slice(c0, c0 + MXU_COLS)
        ya = _dot(v_scr[...], cwo_ref[:, cols]) + cbo_ref[:, cols]
        m_scr[:, cols] = _sigmoid(in_proj(off_ga, c0)) * ya

    rx0 = SUBLANES * rx_halo
    for c0 in range(0, d_rnn, MXU_COLS):
        rx_scr[rx0:rx0 + T, c0:c0 + MXU_COLS] = in_proj(off_rx, c0)
    fill_halo(rx_scr, rxprev_scr, rx_halo)

    def emit_rx(r, lanes, acc):
        xb_scr[r:r + CONV_ROWS, lanes] = acc
        xbb_scr[r:r + CONV_ROWS, lanes] = acc.astype(jnp.bfloat16)
    causal_conv(rx_scr, lcw_ref, lcb_ref, emit_rx)

    lam = lam_ref[...]
    neg_c_sp = -LRU_C * (jnp.maximum(-lam, 0.0) + jnp.log1p(jnp.exp(-jnp.abs(lam))))

    sub = lax.broadcasted_iota(jnp.int32, (SUBLANES, MXU_COLS), 0)
    for j, ks in enumerate(gate_starts):
        c0 = j * MXU_COLS
        cols = slice(c0, c0 + MXU_COLS)
        g = _dot(xbb_scr[:, ks:ks + GATE_K], wg_ref[j])
        r_gate = _sigmoid(g[:, :MXU_COLS] + ba_ref[:, cols])
        i_gate = _sigmoid(g[:, MXU_COLS:] + bx_ref[:, cols])
        log_a = neg_c_sp[:, cols] * r_gate
        a = jnp.exp(log_a)
        mult = jnp.sqrt(-jnp.tanh(log_a) * (1.0 + a * a))
        b = mult * (i_gate * xb_scr[:, cols])

        a3 = a.reshape(L, SUBLANES, MXU_COLS)
        b3 = b.reshape(L, SUBLANES, MXU_COLS)
        hloc, prod = [b3[0]], [a3[0]]
        for i in range(1, L):
            hloc.append(a3[i] * hloc[-1] + b3[i])
            prod.append(a3[i] * prod[-1])
        ea, eb = prod[-1], hloc[-1]
        d = 1
        while d < SUBLANES:
            keep = sub >= d
            ea_sh = jnp.where(keep, pltpu.roll(ea, d, 0), 1.0)
            eb_sh = jnp.where(keep, pltpu.roll(eb, d, 0), 0.0)
            eb = eb + ea * eb_sh
            ea = ea * ea_sh
            d *= 2
        h_in = hcar_scr[:, cols]
        ends = eb + ea * h_in
        start = jnp.where(sub == 0, h_in, pltpu.roll(ends, 1, 0))
        hcar_scr[:, cols] = jnp.broadcast_to(ends[SUBLANES - 1:SUBLANES, :], (SUBLANES, MXU_COLS))
        hseq = jnp.concatenate([hl + pr * start for hl, pr in zip(hloc, prod)], axis=0)

        y_scr[:, cols] = (hseq * _gelu_tanh(in_proj(off_rg, c0))).astype(jnp.bfloat16)

    for c0 in range(0, D, MXU_COLS):
        cols = slice(c0, c0 + MXU_COLS)
        yb = _dot(y_scr[...], lwo_ref[:, cols])
        mb_scr[:, cols] = (m_scr[:, cols] + _sigmoid(in_proj(off_gb, c0)) * yb).astype(jnp.bfloat16)

    for c0 in range(0, D, MXU_COLS):
        res = _dot(mb_scr[...], wmix_ref[:, c0:c0 + MXU_COLS])
        for q in range(MXU_COLS // LANES):
            op_scr[c0 // LANES + q] = res[:, q * LANES:(q + 1) * LANES]
    for r in range(0, T, NORM_ROWS):
        for c in range(D // LANES):
            lanes = slice(c * LANES, (c + 1) * LANES)
            o_ref[r:r + NORM_ROWS, lanes] = (x_ref[r:r + NORM_ROWS, lanes]
                                             + op_scr[c, interleaved_rows(r, NORM_ROWS), :])


def _ffn_kernel(x_ref, n2g_ref, w1_ref, w3_ref, w2_ref, nfg_ref, o_ref, h_scr, act_scr, y_scr):
    tm, D = x_ref.shape
    d_ff = w1_ref.shape[1]

    for r in range(0, tm, NORM_ROWS):
        x = x_ref[r:r + NORM_ROWS, :]
        ms = jnp.mean(x * x, axis=-1, keepdims=True)
        h_scr[r:r + NORM_ROWS, :] = (x * lax.rsqrt(ms + EPS) * n2g_ref[...]).astype(jnp.bfloat16)

    for c0 in range(0, d_ff, LANES):
        cols = slice(c0, c0 + LANES)
        z = _dot(h_scr[...], jnp.concatenate([w1_ref[:, cols], w3_ref[:, cols]], axis=1))
        act_scr[:, cols] = (_silu(z[:, :LANES]) * z[:, LANES:]).astype(jnp.bfloat16)

    for c0 in range(0, D, MXU_COLS):
        cols = slice(c0, c0 + MXU_COLS)
        y_scr[:, cols] = x_ref[:, cols] + _dot(act_scr[...], w2_ref[:, cols])

    for r in range(0, tm, NORM_ROWS):
        y = y_scr[r:r + NORM_ROWS, :]
        ms = jnp.mean(y * y, axis=-1, keepdims=True)
        o_ref[r:r + NORM_ROWS, :] = y * lax.rsqrt(ms + EPS) * nfg_ref[...]


def _resident(shape):
    return pl.BlockSpec(shape, lambda *_: (0,) * len(shape), pipeline_mode=pl.Buffered(1))


def _layer_resident(shape, layer):
    return pl.BlockSpec((None,) + tuple(shape), lambda *_: (layer,) + (0,) * len(shape),
                        pipeline_mode=pl.Buffered(1))


def _stacked_rows(p):
    return p.reshape(p.shape[0], 1, p.shape[1])


def _gate_weights(wa, wx, gate_starts):
    heads, blk, _ = wa.shape
    d_rnn = heads * blk
    head_of = np.arange(d_rnn) // blk
    same_head = jnp.asarray(head_of[:, None] == head_of[None, :])

    def dense(w):
        return jnp.where(same_head, jnp.tile(w.reshape(d_rnn, blk), (1, heads)), 0.0)

    da, dx = dense(wa), dense(wx)
    tiles = []
    for j, ks in enumerate(gate_starts):
        cols = slice(j * MXU_COLS, (j + 1) * MXU_COLS)
        tiles.append(jnp.concatenate([da[ks:ks + GATE_K, cols], dx[ks:ks + GATE_K, cols]], axis=1))
    return jnp.stack(tiles).astype(jnp.bfloat16)


def kernel(x, norm1_g, w_in, b_in, conv_dw_w, conv_dw_b, conv_ln_g, conv_ln_b, conv_w_out, conv_b_out,
           lru_conv_w, lru_conv_b, lru_wa, lru_ba, lru_wx, lru_bx, lru_lambda, lru_w_out, w_mix_out,
           norm2_g, ffn_w1, ffn_w3, ffn_w2, norm_f_g):
    B, S, D = x.shape
    depth = w_in.shape[0]
    d_in = w_in.shape[2]
    conv_k, d_conv = conv_dw_w.shape[1:]
    lru_k, d_rnn = lru_conv_w.shape[1:]
    heads, blk = lru_wa.shape[1:3]
    d_ff = ffn_w1.shape[2]
    T = MIX_TILE
    assert S % T == 0 and (B * S) % FFN_TILE == 0
    assert d_in == 2 * d_conv + 2 * d_rnn + 2 * D and heads * blk == d_rnn
    assert d_conv % MXU_COLS == 0 and d_rnn % MXU_COLS == 0 and D % MXU_COLS == 0 and d_ff % MXU_COLS == 0
    u_halo_rows = SUBLANES * (conv_k - 1)
    rx_halo_rows = SUBLANES * (lru_k - 1)
    gate_starts = _gate_windows(d_rnn, blk)
    n_gate = len(gate_starts)
    bf16 = jnp.bfloat16
    f32 = jnp.float32

    def row(v):
        return v.reshape(1, -1)

    for l in range(depth):
        wg = _gate_weights(lru_wa[l], lru_wx[l], gate_starts)
        mixer = pl.pallas_call(
            functools.partial(_mixer_kernel, gate_starts=gate_starts),
            grid=(B, S // T),
            in_specs=[
                pl.BlockSpec((None, T, D), lambda b, s: (b, s, 0)),
                _layer_resident((1, D), l),
                _resident((D, d_in)), _layer_resident((1, d_in), l),
                _layer_resident((conv_k, d_conv), l), _layer_resident((1, d_conv), l),
                _layer_resident((1, d_conv), l), _layer_resident((1, d_conv), l),
                _resident((d_conv, D)), _layer_resident((1, D), l),
                _layer_resident((lru_k, d_rnn), l), _layer_resident((1, d_rnn), l),
                _resident((n_gate, GATE_K, 2 * MXU_COLS)),
                _layer_resident((1, d_rnn), l), _layer_resident((1, d_rnn), l), _layer_resident((1, d_rnn), l),
                _resident((d_rnn, D)), _resident((D, D)),
            ],
            out_specs=pl.BlockSpec((None, T, D), lambda b, s: (b, s, 0)),
            out_shape=jax.ShapeDtypeStruct((B, S, D), f32),
            scratch_shapes=[
                pltpu.VMEM((D // LANES, T, LANES), f32),
                pltpu.VMEM((T, D), bf16),
                pltpu.VMEM((u_halo_rows + T, d_conv), f32),
                pltpu.VMEM((u_halo_rows, d_conv), f32),
                pltpu.VMEM((T, d_conv), f32),
                pltpu.VMEM((T, d_conv), bf16),
                pltpu.VMEM((rx_halo_rows + T, d_rnn), f32),
                pltpu.VMEM((rx_halo_rows, d_rnn), f32),
                pltpu.VMEM((T, d_rnn), f32),
                pltpu.VMEM((T, d_rnn), bf16),
                pltpu.VMEM((T, d_rnn), bf16),
                pltpu.VMEM((T, D), f32),
                pltpu.VMEM((T, D), bf16),
                pltpu.VMEM((D // LANES, T, LANES), f32),
                pltpu.VMEM((SUBLANES, d_rnn), f32),
            ],
            compiler_params=pltpu.CompilerParams(
                dimension_semantics=("arbitrary", "arbitrary"),
                vmem_limit_bytes=MIX_VMEM_BYTES),
            name="token_mixer",
        )
        x = mixer(
            x, _stacked_rows(norm1_g), w_in[l].astype(bf16), _stacked_rows(b_in),
            conv_dw_w, _stacked_rows(conv_dw_b), _stacked_rows(conv_ln_g), _stacked_rows(conv_ln_b),
            conv_w_out[l].astype(bf16), _stacked_rows(conv_b_out),
            lru_conv_w, _stacked_rows(lru_conv_b), wg, _stacked_rows(lru_ba), _stacked_rows(lru_bx),
            _stacked_rows(lru_lambda),
            lru_w_out[l].astype(bf16), w_mix_out[l].astype(bf16))

        last = l == depth - 1
        ffn = pl.pallas_call(
            _ffn_kernel,
            grid=(B * S // FFN_TILE,),
            in_specs=[
                pl.BlockSpec((FFN_TILE, D), lambda i: (i, 0)),
                _layer_resident((1, D), l),
                _resident((D, d_ff)), _resident((D, d_ff)), _resident((d_ff, D)),
                _resident((1, D)),
            ],
            out_specs=pl.BlockSpec((FFN_TILE, D), lambda i: (i, 0)),
            out_shape=jax.ShapeDtypeStruct((B * S, D), f32),
            scratch_shapes=[
                pltpu.VMEM((FFN_TILE, D), bf16),
                pltpu.VMEM((FFN_TILE, d_ff), bf16),
                pltpu.VMEM((FFN_TILE, D), f32),
            ],
            compiler_params=pltpu.CompilerParams(
                dimension_semantics=("arbitrary",),
                vmem_limit_bytes=FFN_VMEM_BYTES),
            name="channel_mixer",
        )
        assert last, "the final rmsnorm is fused into the last layer's channel mixer"
        x = ffn(x.reshape(B * S, D), _stacked_rows(norm2_g), ffn_w1[l].astype(bf16), ffn_w3[l].astype(bf16),
                ffn_w2[l].astype(bf16), row(norm_f_g)).reshape(B, S, D)
    return x
```

```python
import functools
import math

import jax
import jax.numpy as jnp
import numpy as np
from jax import lax
from jax.experimental import pallas as pl
from jax.experimental.pallas import tpu as pltpu

EPS = 1e-6
LRU_C = 8.0

SUBLANES = 8
LANES = 128
MXU_COLS = 256

MIX_TILE = 512
FFN_TILE = 1024
CONV_ROWS = 128
CONV_LAG = 1
NORM_ROWS = 32
GATE_K = 384
GATE_COLS = 128

MIX_VMEM_BYTES = 60 * 1024 * 1024
FFN_VMEM_BYTES = 56 * 1024 * 1024


def _sigmoid(x):
    return 0.5 * jnp.tanh(0.5 * x) + 0.5


def _silu(x):
    return x * _sigmoid(x)


def _gelu_tanh(x):
    c = math.sqrt(2.0 / math.pi)
    return 0.5 * x * (1.0 + jnp.tanh(c * (x + 0.044715 * (x * x * x))))


def _dot(a, b):
    return jnp.dot(a, b, preferred_element_type=jnp.float32)


def _gate_windows(d_rnn, block):
    starts = []
    for j in range(d_rnn // GATE_COLS):
        lo = (j * GATE_COLS // block) * block
        hi = -(-((j + 1) * GATE_COLS) // block) * block
        start = min(lo // LANES * LANES, d_rnn - GATE_K)
        assert start <= lo and hi <= start + GATE_K
        starts.append(start)
    return tuple(starts)


def _mixer_kernel(x_ref, n1g_ref, win_ref, bin_ref, cw_ref, cb_ref, lng_ref, lnb_ref, cwo_ref, cbo_ref,
                  lcw_ref, lcb_ref, wg_ref, ba_ref, bx_ref, lam_ref, lwo_ref, wmix_ref,
                  o_ref,
                  hp_scr, h_scr, u_scr, uprev_scr, c_scr, v_scr, rx_scr, rxprev_scr, xb_scr, xbb_scr,
                  y_scr, m_scr, mb_scr, op_scr, hcar_scr,
                  *, gate_starts):
    T, D = x_ref.shape
    L = T // SUBLANES
    d_conv = cwo_ref.shape[0]
    d_rnn = lwo_ref.shape[0]
    u_halo = cw_ref.shape[0] - 1
    rx_halo = lcw_ref.shape[0] - 1
    assert u_halo <= L and rx_halo <= L and L % NORM_ROWS == 0 and T % CONV_ROWS == 0
    off_glu_b = d_conv
    off_rx = 2 * d_conv
    off_rg = off_rx + d_rnn
    off_ga = off_rg + d_rnn
    off_gb = off_ga + D

    @pl.when(pl.program_id(1) == 0)
    def _():
        uprev_scr[...] = jnp.zeros(uprev_scr.shape, jnp.float32)
        rxprev_scr[...] = jnp.zeros(rxprev_scr.shape, jnp.float32)
        hcar_scr[...] = jnp.zeros(hcar_scr.shape, jnp.float32)

    def interleaved_rows(r, n):
        s, i0 = divmod(r, L)
        return pl.ds(SUBLANES * i0 + s, n, stride=SUBLANES)

    for r in range(0, T, NORM_ROWS):
        x = x_ref[r:r + NORM_ROWS, :]
        ms = jnp.mean(x * x, axis=-1, keepdims=True)
        h = x * lax.rsqrt(ms + EPS) * n1g_ref[...]
        for c in range(D // LANES):
            hp_scr[c, interleaved_rows(r, NORM_ROWS), :] = h[:, c * LANES:(c + 1) * LANES]
    for c in range(D // LANES):
        h_scr[:, c * LANES:(c + 1) * LANES] = hp_scr[c].astype(jnp.bfloat16)

    def in_proj(off, c0, width=MXU_COLS):
        cols = slice(off + c0, off + c0 + width)
        return _dot(h_scr[...], win_ref[:, cols]) + bin_ref[:, cols]

    def fill_halo(buf, prev, halo):
        n, width = SUBLANES * halo, buf.shape[1]
        tail = buf[T:T + n, :].reshape(halo, SUBLANES, width)
        sub = lax.broadcasted_iota(jnp.int32, tail.shape, 1)
        mixed = jnp.where(sub == SUBLANES - 1, prev[...].reshape(tail.shape), tail)
        buf[0:n, :] = pltpu.roll(mixed, 1, 1).reshape(n, width)
        prev[...] = tail.reshape(n, width)

    always = pl.program_id(1) >= 0

    def causal_conv(buf, w_ref, b_ref, emit):
        groups = CONV_ROWS // SUBLANES
        prev = None
        for l0 in range(0, buf.shape[1], LANES):
            lanes = slice(l0, l0 + LANES)
            for r in range(0, T, CONV_ROWS):
                acc = jnp.broadcast_to(b_ref[:, lanes], (groups, SUBLANES, LANES))
                if prev is not None:
                    acc = jnp.where(always, acc, prev)
                sums = [acc[0]]
                for k in range(w_ref.shape[0]):
                    row = r + SUBLANES * k
                    wk = jnp.broadcast_to(w_ref[k:k + 1, lanes], (SUBLANES, LANES))
                    if k >= CONV_LAG:
                        wk = jnp.where(always, wk, sums[k - CONV_LAG])
                    acc = acc + wk[None] * buf[row:row + CONV_ROWS, lanes].reshape(groups, SUBLANES, LANES)
                    sums.append(acc[0])
                emit(r, lanes, acc.reshape(CONV_ROWS, LANES))
                prev = acc

    u0 = SUBLANES * u_halo
    for c0 in range(0, d_conv, MXU_COLS):
        za = in_proj(0, c0)
        zb = in_proj(off_glu_b, c0)
        u_scr[u0:u0 + T, c0:c0 + MXU_COLS] = za * _sigmoid(zb)
    fill_halo(u_scr, uprev_scr, u_halo)

    def emit_conv(r, lanes, acc):
        c_scr[r:r + CONV_ROWS, lanes] = acc
    causal_conv(u_scr, cw_ref, cb_ref, emit_conv)

    for r in range(0, T, NORM_ROWS):
        cv = c_scr[r:r + NORM_ROWS, :]
        mu = jnp.mean(cv, axis=-1, keepdims=True)
        xc = cv - mu
        var = jnp.mean(xc * xc, axis=-1, keepdims=True)
        y = xc * lax.rsqrt(var + EPS) * lng_ref[...] + lnb_ref[...]
        v_scr[r:r + NORM_ROWS, :] = _silu(y).astype(jnp.bfloat16)

    for c0 in range(0, D, MXU_COLS):
        cols = slice(c0, c0 + MXU_COLS)
        ya = _dot(v_scr[...], cwo_ref[:, cols]) + cbo_ref[:, cols]
        m_scr[:, cols] = _sigmoid(in_proj(off_ga, c0)) * ya

    rx0 = SUBLANES * rx_halo
    for c0 in range(0, d_rnn, MXU_COLS):
        rx_scr[rx0:rx0 + T, c0:c0 + MXU_COLS] = in_proj(off_rx, c0)
    fill_halo(rx_scr, rxprev_scr, rx_halo)

    def emit_rx(r, lanes, acc):
        xb_scr[r:r + CONV_ROWS, lanes] = acc
        xbb_scr[r:r + CONV_ROWS, lanes] = acc.astype(jnp.bfloat16)
    causal_conv(rx_scr, lcw_ref, lcb_ref, emit_rx)

    lam = lam_ref[...]
    neg_c_sp = -LRU_C * (jnp.maximum(-lam, 0.0) + jnp.log1p(jnp.exp(-jnp.abs(lam))))

    sub = lax.broadcasted_iota(jnp.int32, (SUBLANES, GATE_COLS), 0)
    per_chunk = MXU_COLS // GATE_COLS
    for j, ks in enumerate(gate_starts):
        c0 = j * GATE_COLS
        cols = slice(c0, c0 + GATE_COLS)
        if j % per_chunk == 0:
            gelu_gate = _gelu_tanh(in_proj(off_rg, c0))
        g = _dot(xbb_scr[:, ks:ks + GATE_K], wg_ref[j])
        r_gate = _sigmoid(g[:, :GATE_COLS] + ba_ref[:, cols])
        i_gate = _sigmoid(g[:, GATE_COLS:] + bx_ref[:, cols])
        log_a = neg_c_sp[:, cols] * r_gate
        a = jnp.exp(log_a)
        mult = jnp.sqrt(-jnp.tanh(log_a) * (1.0 + a * a))
        b = mult * (i_gate * xb_scr[:, cols])

        a3 = a.reshape(L, SUBLANES, GATE_COLS)
        b3 = b.reshape(L, SUBLANES, GATE_COLS)
        hloc, prod = [b3[0]], [a3[0]]
        for i in range(1, L):
            hloc.append(a3[i] * hloc[-1] + b3[i])
            prod.append(a3[i] * prod[-1])
        ea, eb = prod[-1], hloc[-1]
        d = 1
        while d < SUBLANES:
            keep = sub >= d
            ea_sh = jnp.where(keep, pltpu.roll(ea, d, 0), 1.0)
            eb_sh = jnp.where(keep, pltpu.roll(eb, d, 0), 0.0)
            eb = eb + ea * eb_sh
            ea = ea * ea_sh
            d *= 2
        h_in = hcar_scr[:, cols]
        ends = eb + ea * h_in
        start = jnp.where(sub == 0, h_in, pltpu.roll(ends, 1, 0))
        hcar_scr[:, cols] = jnp.broadcast_to(ends[SUBLANES - 1:SUBLANES, :], (SUBLANES, GATE_COLS))
        hseq = jnp.concatenate([hl + pr * start for hl, pr in zip(hloc, prod)], axis=0)

        q = (j % per_chunk) * GATE_COLS
        y_scr[:, cols] = (hseq * gelu_gate[:, q:q + GATE_COLS]).astype(jnp.bfloat16)

    for c0 in range(0, D, MXU_COLS):
        cols = slice(c0, c0 + MXU_COLS)
        yb = _dot(y_scr[...], lwo_ref[:, cols])
        mb_scr[:, cols] = (m_scr[:, cols] + _sigmoid(in_proj(off_gb, c0)) * yb).astype(jnp.bfloat16)

    for c0 in range(0, D, MXU_COLS):
        res = _dot(mb_scr[...], wmix_ref[:, c0:c0 + MXU_COLS])
        for q in range(MXU_COLS // LANES):
            op_scr[c0 // LANES + q] = res[:, q * LANES:(q + 1) * LANES]
    for r in range(0, T, NORM_ROWS):
        for c in range(D // LANES):
            lanes = slice(c * LANES, (c + 1) * LANES)
            o_ref[r:r + NORM_ROWS, lanes] = (x_ref[r:r + NORM_ROWS, lanes]
                                             + op_scr[c, interleaved_rows(r, NORM_ROWS), :])


def _ffn_kernel(x_ref, n2g_ref, w1_ref, w3_ref, w2_ref, nfg_ref, o_ref, h_scr, act_scr, y_scr):
    tm, D = x_ref.shape
    d_ff = w1_ref.shape[1]

    for r in range(0, tm, NORM_ROWS):
        x = x_ref[r:r + NORM_ROWS, :]
        ms = jnp.mean(x * x, axis=-1, keepdims=True)
        h_scr[r:r + NORM_ROWS, :] = (x * lax.rsqrt(ms + EPS) * n2g_ref[...]).astype(jnp.bfloat16)

    for c0 in range(0, d_ff, MXU_COLS):
        cols = slice(c0, c0 + MXU_COLS)
        g = _dot(h_scr[...], w1_ref[:, cols])
        u = _dot(h_scr[...], w3_ref[:, cols])
        act_scr[:, cols] = (_silu(g) * u).astype(jnp.bfloat16)

    for c0 in range(0, D, MXU_COLS):
        cols = slice(c0, c0 + MXU_COLS)
        y_scr[:, cols] = x_ref[:, cols] + _dot(act_scr[...], w2_ref[:, cols])

    for r in range(0, tm, NORM_ROWS):
        y = y_scr[r:r + NORM_ROWS, :]
        ms = jnp.mean(y * y, axis=-1, keepdims=True)
        o_ref[r:r + NORM_ROWS, :] = y * lax.rsqrt(ms + EPS) * nfg_ref[...]


def _resident(shape):
    return pl.BlockSpec(shape, lambda *_: (0,) * len(shape), pipeline_mode=pl.Buffered(1))


def _layer_resident(shape, layer):
    return pl.BlockSpec((None,) + tuple(shape), lambda *_: (layer,) + (0,) * len(shape),
                        pipeline_mode=pl.Buffered(1))


def _stacked_rows(p):
    return p.reshape(p.shape[0], 1, p.shape[1])


def _gate_weights(wa, wx, gate_starts):
    heads, blk, _ = wa.shape
    d_rnn = heads * blk
    head_of = np.arange(d_rnn) // blk
    same_head = jnp.asarray(head_of[:, None] == head_of[None, :])

    def dense(w):
        return jnp.where(same_head, jnp.tile(w.reshape(d_rnn, blk), (1, heads)), 0.0)

    da, dx = dense(wa), dense(wx)
    tiles = []
    for j, ks in enumerate(gate_starts):
        cols = slice(j * GATE_COLS, (j + 1) * GATE_COLS)
        tiles.append(jnp.concatenate([da[ks:ks + GATE_K, cols], dx[ks:ks + GATE_K, cols]], axis=1))
    return jnp.stack(tiles).astype(jnp.bfloat16)


def kernel(x, norm1_g, w_in, b_in, conv_dw_w, conv_dw_b, conv_ln_g, conv_ln_b, conv_w_out, conv_b_out,
           lru_conv_w, lru_conv_b, lru_wa, lru_ba, lru_wx, lru_bx, lru_lambda, lru_w_out, w_mix_out,
           norm2_g, ffn_w1, ffn_w3, ffn_w2, norm_f_g):
    B, S, D = x.shape
    depth = w_in.shape[0]
    d_in = w_in.shape[2]
    conv_k, d_conv = conv_dw_w.shape[1:]
    lru_k, d_rnn = lru_conv_w.shape[1:]
    heads, blk = lru_wa.shape[1:3]
    d_ff = ffn_w1.shape[2]
    T = MIX_TILE
    assert S % T == 0 and (B * S) % FFN_TILE == 0
    assert d_in == 2 * d_conv + 2 * d_rnn + 2 * D and heads * blk == d_rnn
    assert d_conv % MXU_COLS == 0 and d_rnn % MXU_COLS == 0 and D % MXU_COLS == 0 and d_ff % MXU_COLS == 0
    u_halo_rows = SUBLANES * (conv_k - 1)
    rx_halo_rows = SUBLANES * (lru_k - 1)
    gate_starts = _gate_windows(d_rnn, blk)
    n_gate = len(gate_starts)
    bf16 = jnp.bfloat16
    f32 = jnp.float32

    def row(v):
        return v.reshape(1, -1)

    for l in range(depth):
        wg = _gate_weights(lru_wa[l], lru_wx[l], gate_starts)
        mixer = pl.pallas_call(
            functools.partial(_mixer_kernel, gate_starts=gate_starts),
            grid=(B, S // T),
            in_specs=[
                pl.BlockSpec((None, T, D), lambda b, s: (b, s, 0)),
                _layer_resident((1, D), l),
                _resident((D, d_in)), _layer_resident((1, d_in), l),
                _layer_resident((conv_k, d_conv), l), _layer_resident((1, d_conv), l),
                _layer_resident((1, d_conv), l), _layer_resident((1, d_conv), l),
                _resident((d_conv, D)), _layer_resident((1, D), l),
                _layer_resident((lru_k, d_rnn), l), _layer_resident((1, d_rnn), l),
                _resident((n_gate, GATE_K, 2 * GATE_COLS)),
                _layer_resident((1, d_rnn), l), _layer_resident((1, d_rnn), l), _layer_resident((1, d_rnn), l),
                _resident((d_rnn, D)), _resident((D, D)),
            ],
            out_specs=pl.BlockSpec((None, T, D), lambda b, s: (b, s, 0)),
            out_shape=jax.ShapeDtypeStruct((B, S, D), f32),
            scratch_shapes=[
                pltpu.VMEM((D // LANES, T, LANES), f32),
                pltpu.VMEM((T, D), bf16),
                pltpu.VMEM((u_halo_rows + T, d_conv), f32),
                pltpu.VMEM((u_halo_rows, d_conv), f32),
                pltpu.VMEM((T, d_conv), f32),
                pltpu.VMEM((T, d_conv), bf16),
                pltpu.VMEM((rx_halo_rows + T, d_rnn), f32),
                pltpu.VMEM((rx_halo_rows, d_rnn), f32),
                pltpu.VMEM((T, d_rnn), f32),
                pltpu.VMEM((T, d_rnn), bf16),
                pltpu.VMEM((T, d_rnn), bf16),
                pltpu.VMEM((T, D), f32),
                pltpu.VMEM((T, D), bf16),
                pltpu.VMEM((D // LANES, T, LANES), f32),
                pltpu.VMEM((SUBLANES, d_rnn), f32),
            ],
            compiler_params=pltpu.CompilerParams(
                dimension_semantics=("arbitrary", "arbitrary"),
                vmem_limit_bytes=MIX_VMEM_BYTES),
            name="token_mixer",
        )
        x = mixer(
            x, _stacked_rows(norm1_g), w_in[l].astype(bf16), _stacked_rows(b_in),
            conv_dw_w, _stacked_rows(conv_dw_b), _stacked_rows(conv_ln_g), _stacked_rows(conv_ln_b),
            conv_w_out[l].astype(bf16), _stacked_rows(conv_b_out),
            lru_conv_w, _stacked_rows(lru_conv_b), wg, _stacked_rows(lru_ba), _stacked_rows(lru_bx),
            _stacked_rows(lru_lambda),
            lru_w_out[l].astype(bf16), w_mix_out[l].astype(bf16))

        last = l == depth - 1
        ffn = pl.pallas_call(
            _ffn_kernel,
            grid=(B * S // FFN_TILE,),
            in_specs=[
                pl.BlockSpec((FFN_TILE, D), lambda i: (i, 0)),
                _layer_resident((1, D), l),
                _resident((D, d_ff)), _resident((D, d_ff)), _resident((d_ff, D)),
                _resident((1, D)),
            ],
            out_specs=pl.BlockSpec((FFN_TILE, D), lambda i: (i, 0)),
            out_shape=jax.ShapeDtypeStruct((B * S, D), f32),
            scratch_shapes=[
                pltpu.VMEM((FFN_TILE, D), bf16),
                pltpu.VMEM((FFN_TILE, d_ff), bf16),
                pltpu.VMEM((FFN_TILE, D), f32),
            ],
            compiler_params=pltpu.CompilerParams(
                dimension_semantics=("arbitrary",),
                vmem_limit_bytes=FFN_VMEM_BYTES),
            name="channel_mixer",
        )
        assert last, "the final rmsnorm is fused into the last layer's channel mixer"
        x = ffn(x.reshape(B * S, D), _stacked_rows(norm2_g), ffn_w1[l].astype(bf16), ffn_w3[l].astype(bf16),
                ffn_w2[l].astype(bf16), row(norm_f_g)).reshape(B, S, D)
    return x
```
